```python
import jax, jax.numpy as jnp
from jax import lax
import numpy as np

D_MODEL = 4096
BATCH = 2
SEQ = 4096
DEPTH = 2

D_CONF = D_MODEL // 2
D_SHORT = D_MODEL // 2
CONF_KERNEL = 31
SHORT_KERNEL = 3
SPLITS = [D_CONF, 2 * D_CONF, 2 * D_CONF + D_SHORT, 2 * D_CONF + 2 * D_SHORT,
          2 * D_CONF + 3 * D_SHORT, 2 * D_CONF + 3 * D_SHORT + D_MODEL]
IN_COLS = 2 * D_CONF + 3 * D_SHORT + 2 * D_MODEL
N_EXPERTS = 16
N_GROUPS = 4
EXPERTS_PER_GROUP = N_EXPERTS // N_GROUPS
TOP_K = 2
D_FF = D_MODEL // 4
LN_EPS = 1e-5
DEEPNORM_ALPHA = (2 * DEPTH) ** 0.25
DEEPNORM_BETA = (8 * DEPTH) ** -0.25
N_MOD = 6

kernel_name = "hybrid_conv_moe_deepnorm_adaln"


def layer_norm(x, g, b):
    xf = x.astype(jnp.float32)
    mu = jnp.mean(xf, axis=-1, keepdims=True)
    var = jnp.mean(jnp.square(xf - mu), axis=-1, keepdims=True)
    y = (xf - mu) * lax.rsqrt(var + LN_EPS)
    return (y * g.astype(jnp.float32) + b.astype(jnp.float32)).astype(x.dtype)


def causal_depthwise_conv(x, w):
    k = w.shape[0]
    return lax.conv_general_dilated(
        x, w[:, None, :].astype(x.dtype), window_strides=(1,), padding=((k - 1, 0),),
        dimension_numbers=("NWC", "WIO", "NWC"), feature_group_count=x.shape[-1])


def hybrid_mixer(h, w_in, conv_a_w, conv_a_b, ln_a_g, ln_a_b, conv_b_w, w_pa, w_pb, w_o):
    u = jnp.einsum('bsd,dn->bsn', h, w_in)
    a_val, a_gate, s_b, s_c, s_x, g_a, g_b = jnp.split(u, SPLITS, axis=-1)
    ya = a_val * jax.nn.sigmoid(a_gate)
    ya = causal_depthwise_conv(ya, conv_a_w) + conv_a_b
    ya = jax.nn.silu(layer_norm(ya, ln_a_g, ln_a_b))
    pa = jnp.einsum('bsc,cd->bsd', ya, w_pa)
    yb = s_b * causal_depthwise_conv(s_c * s_x, conv_b_w)
    pb = jnp.einsum('bsc,cd->bsd', yb, w_pb)
    m = jax.nn.sigmoid(g_a) * pa + jax.nn.sigmoid(g_b) * pb
    return jnp.einsum('bsd,de->bse', m, w_o)


def grouped_moe(h, w_router, router_bias, w_gate, w_up, w_down):
    logits = jnp.einsum('bsd,de->bse', h, w_router).astype(jnp.float32)
    probs = jax.nn.softmax(logits, axis=-1)
    sel = probs + router_bias.astype(jnp.float32)
    grouped = sel.reshape(sel.shape[:-1] + (N_GROUPS, EXPERTS_PER_GROUP))
    group_score = jnp.sum(lax.top_k(grouped, TOP_K)[0], axis=-1)
    best_group = jnp.argmax(group_score, axis=-1)
    expert_group = jnp.arange(N_EXPERTS) // EXPERTS_PER_GROUP
    in_group = expert_group == best_group[..., None]
    masked = jnp.where(in_group, sel, -jnp.inf)
    _, top_idx = lax.top_k(masked, TOP_K)
    top_p = jnp.take_along_axis(probs, top_idx, axis=-1)
    top_w = top_p / jnp.sum(top_p, axis=-1, keepdims=True)
    combine = jnp.sum(jax.nn.one_hot(top_idx, N_EXPERTS, dtype=jnp.float32) * top_w[..., None],
                      axis=-2).astype(h.dtype)
    out = jnp.zeros_like(h)
    for e in range(N_EXPERTS):
        hid = jax.nn.silu(h @ w_gate[e]) * (h @ w_up[e])
        out = out + combine[..., e:e + 1] * (hid @ w_down[e])
    return out


def setup_inputs(seed: int = 0) -> dict:
    key = jax.random.key(seed)
    ks = jax.random.split(key, 24)
    f32 = jnp.float32
    nrm = lambda k, shape, s: jax.random.normal(k, shape, f32) * s
    L, D = DEPTH, D_MODEL
    return {
        "x": nrm(ks[0], (BATCH, SEQ, D), 1.0),
        "c": nrm(ks[1], (BATCH, D), 1.0),
        "w_ada": nrm(ks[2], (L, D, N_MOD * D), 0.5 * D ** -0.5),
        "b_ada": nrm(ks[3], (L, N_MOD * D), 0.02),
        "w_in": nrm(ks[4], (L, D, IN_COLS), D ** -0.5),
        "conv_a_w": nrm(ks[5], (L, CONF_KERNEL, D_CONF), CONF_KERNEL ** -0.5),
        "conv_a_b": nrm(ks[6], (L, D_CONF), 0.01),
        "ln_a_g": 1.0 + nrm(ks[7], (L, D_CONF), 0.02),
        "ln_a_b": nrm(ks[8], (L, D_CONF), 0.02),
        "conv_b_w": nrm(ks[9], (L, SHORT_KERNEL, D_SHORT), SHORT_KERNEL ** -0.5),
        "w_pa": nrm(ks[10], (L, D_CONF, D), DEEPNORM_BETA * D_CONF ** -0.5),
        "w_pb": nrm(ks[11], (L, D_SHORT, D), DEEPNORM_BETA * D_SHORT ** -0.5),
        "w_o": nrm(ks[12], (L, D, D), DEEPNORM_BETA * D ** -0.5),
        "ln_mix_g": 1.0 + nrm(ks[13], (L, D), 0.02),
        "ln_mix_b": nrm(ks[14], (L, D), 0.02),
        "w_router": nrm(ks[15], (D, N_EXPERTS), D ** -0.5),
        "router_bias": nrm(ks[16], (N_EXPERTS,), 0.01),
        "w_gate": nrm(ks[17], (L, N_EXPERTS, D, D_FF), D ** -0.5),
        "w_up": nrm(ks[18], (L, N_EXPERTS, D, D_FF), D ** -0.5),
        "w_down": nrm(ks[19], (L, N_EXPERTS, D_FF, D), DEEPNORM_BETA * D_FF ** -0.5),
        "ln_ffn_g": 1.0 + nrm(ks[20], (L, D), 0.02),
        "ln_ffn_b": nrm(ks[21], (L, D), 0.02),
    }


def reference(x, c, w_ada, b_ada, w_in, conv_a_w, conv_a_b, ln_a_g, ln_a_b, conv_b_w,
              w_pa, w_pb, w_o, ln_mix_g, ln_mix_b, w_router, router_bias,
              w_gate, w_up, w_down, ln_ffn_g, ln_ffn_b):
    cond = jax.nn.silu(c)
    for l in range(DEPTH):
        mod = cond @ w_ada[l] + b_ada[l]
        sh_m, sc_m, g_m, sh_f, sc_f, g_f = jnp.split(mod[:, None, :], N_MOD, axis=-1)
        h = x * (1 + sc_m) + sh_m
        y = hybrid_mixer(h, w_in[l], conv_a_w[l], conv_a_b[l], ln_a_g[l], ln_a_b[l],
                         conv_b_w[l], w_pa[l], w_pb[l], w_o[l])
        x = layer_norm(DEEPNORM_ALPHA * x + g_m * y, ln_mix_g[l], ln_mix_b[l])
        h = x * (1 + sc_f) + sh_f
        y = grouped_moe(h, w_router, router_bias, w_gate[l], w_up[l], w_down[l])
        x = layer_norm(DEEPNORM_ALPHA * x + g_f * y, ln_ffn_g[l], ln_ffn_b[l])
    return x
```

```python
import functools

import jax
import jax.numpy as jnp
from jax import lax
from jax.experimental import pallas as pl
from jax.experimental.pallas import tpu as pltpu

F32 = jnp.float32
BF16 = jnp.bfloat16
I32 = jnp.int32

LN_EPS = 1e-5
N_GROUPS = 4
N_MOD = 6
V7X_VMEM_LIMIT_BYTES = 60000 * 1024
LANES = 128
SUBLANES = 8
HALO_ROWS = 32


def _params(n_grid):
    return pltpu.CompilerParams(
        dimension_semantics=("arbitrary",) * n_grid, vmem_limit_bytes=V7X_VMEM_LIMIT_BYTES)


def _sigmoid(v):
    return 1.0 / (1.0 + jnp.exp(-v))


def _silu(v):
    return v * _sigmoid(v)


def _layer_norm(z, g, b):
    mu = jnp.mean(z, axis=-1, keepdims=True)
    zc = z - mu
    var = jnp.mean(zc * zc, axis=-1, keepdims=True)
    return zc * lax.rsqrt(var + LN_EPS) * g + b


def _mod_kernel(c_ref, w_ref, b_ref, o_ref):
    cond = _silu(c_ref[...])
    acc = jnp.dot(cond.astype(BF16), w_ref[...].astype(BF16), preferred_element_type=F32)
    o_ref[...] = acc + b_ref[...]


def _modulation(c, w_ada, b_ada, bn):
    n_layers, d, n = w_ada.shape
    b = c.shape[0]
    c_pad = jnp.zeros((SUBLANES, d), F32).at[:b].set(c)
    out = pl.pallas_call(
        _mod_kernel,
        grid=(n_layers, n // bn),
        in_specs=[
            pl.BlockSpec((SUBLANES, d), lambda l, j: (0, 0)),
            pl.BlockSpec((None, d, bn), lambda l, j: (l, 0, j)),
            pl.BlockSpec((None, 1, bn), lambda l, j: (l, 0, j)),
        ],
        out_specs=pl.BlockSpec((None, SUBLANES, bn), lambda l, j: (l, 0, j)),
        out_shape=jax.ShapeDtypeStruct((n_layers, SUBLANES, n), F32),
        compiler_params=_params(2),
        name="adaln_mod",
    )(c_pad, w_ada, b_ada.reshape(n_layers, 1, n))
    return out[:, :b].reshape(n_layers, b, N_MOD, d)


def _modulate_kernel(x_ref, mod_ref, h_ref, *, shift_row, scale_row):
    sc = mod_ref[scale_row:scale_row + 1, :]
    sh = mod_ref[shift_row:shift_row + 1, :]
    h_ref[...] = (x_ref[...] * (1.0 + sc) + sh).astype(BF16)


def _modulate(x2, mod, layer, seq, tm):
    t, d = x2.shape
    tiles_per_seq = seq // tm
    return pl.pallas_call(
        functools.partial(_modulate_kernel, shift_row=0, scale_row=1),
        grid=(t // tm,),
        in_specs=[
            pl.BlockSpec((tm, d), lambda i: (i, 0)),
            pl.BlockSpec((None, None, N_MOD, d), lambda i: (layer, i // tiles_per_seq, 0, 0)),
        ],
        out_specs=pl.BlockSpec((tm, d), lambda i: (i, 0)),
        out_shape=jax.ShapeDtypeStruct((t, d), BF16),
        compiler_params=_params(1),
        name="modulate_in",
    )(x2, mod)


def _mm_kernel(*refs, n_x, w_x, n_e, epilogue):
    n_w = len(w_x)
    x_refs = refs[:n_x]
    w_refs = refs[n_x:n_x + n_w]
    e_refs = refs[n_x + n_w:n_x + n_w + n_e]
    o_ref, wb_ref = refs[-2], refs[-1]

    @pl.when(pl.program_id(1) == 0)
    def _cast_weights():
        for i in range(n_w):
            wb_ref[i] = w_refs[i][...].astype(BF16)

    accs = [jnp.dot(x_refs[w_x[i]][...], wb_ref[i], preferred_element_type=F32) for i in range(n_w)]
    o_ref[...] = epilogue(accs, [e[...] for e in e_refs]).astype(o_ref.dtype)


def _matmul(xs, ws, extras, epilogue, n_out, out_dtype, bm, bn, name):
    m, k = xs[0].shape
    n_w = len(ws)
    in_specs = [pl.BlockSpec((bm, x.shape[1]), lambda j, i: (i, 0)) for x in xs]
    args = list(xs)
    for w, lead, col0, _ in ws:
        assert col0 % bn == 0 and w.shape[1] == k
        in_specs.append(pl.BlockSpec((None, k, bn), functools.partial(
            lambda j, i, lead, cb: (lead, 0, cb + j), lead=lead, cb=col0 // bn)))
        args.append(w)
    for e, col0 in extras:
        assert col0 % bn == 0
        in_specs.append(pl.BlockSpec((bm, bn), functools.partial(
            lambda j, i, cb: (i, cb + j), cb=col0 // bn)))
        args.append(e)
    return pl.pallas_call(
        functools.partial(_mm_kernel, n_x=len(xs), w_x=tuple(w[3] for w in ws), n_e=len(extras),
                          epilogue=epilogue),
        grid=(n_out // bn, m // bm),
        in_specs=in_specs,
        out_specs=pl.BlockSpec((bm, bn), lambda j, i: (i, j)),
        out_shape=jax.ShapeDtypeStruct((m, n_out), out_dtype),
        scratch_shapes=[pltpu.VMEM((n_w, k, bn), BF16)],
        compiler_params=_params(2),
        name=name,
    )(*args)


def _ep_glu(accs, _):
    return accs[0] * _sigmoid(accs[1])


def _ep_mul(accs, _):
    return accs[0] * accs[1]


def _ep_id(accs, _):
    return accs[0]


def _ep_sigmoid(accs, _):
    return _sigmoid(accs[0])


def _ep_merge(accs, extras):
    return extras[0].astype(F32) * accs[0] + extras[1].astype(F32) * accs[1]


def _conv_kernel(*refs, taps, tm, row_chunk, lane_chunk, tiles_per_seq, mode):
    if mode == "ln_silu":
        cur_ref, halo_ref, w_ref, b_ref, g_ref, beta_ref, o_ref, buf_ref, y_ref = refs
    else:
        cur_ref, halo_ref, w_ref, gate_ref, o_ref, buf_ref = refs
    c = cur_ref.shape[1]
    first = (pl.program_id(0) % tiles_per_seq) == 0
    buf_ref[0:HALO_ROWS, :] = jnp.where(first, 0.0, halo_ref[...])
    buf_ref[HALO_ROWS:HALO_ROWS + tm, :] = cur_ref[...]
    base = HALO_ROWS - (taps - 1)
    for r0 in range(0, tm, row_chunk):
        for c0 in range(0, c, lane_chunk):
            acc = jnp.zeros((row_chunk, lane_chunk), F32)
            for kk in range(taps):
                xk = buf_ref[base + kk + r0:base + kk + r0 + row_chunk, c0:c0 + lane_chunk]
                acc = acc + w_ref[kk:kk + 1, c0:c0 + lane_chunk] * xk
            if mode == "ln_silu":
                y_ref[r0:r0 + row_chunk, c0:c0 + lane_chunk] = acc + b_ref[:, c0:c0 + lane_chunk]
            else:
                gate = gate_ref[r0:r0 + row_chunk, c0:c0 + lane_chunk].astype(F32)
                o_ref[r0:r0 + row_chunk, c0:c0 + lane_chunk] = (gate * acc).astype(o_ref.dtype)
    if mode == "ln_silu":
        o_ref[...] = _silu(_layer_norm(y_ref[...], g_ref[...], beta_ref[...])).astype(o_ref.dtype)


def _causal_conv(x, w, seq, tm, mode, extra, name):
    t, c = x.shape
    taps = w.shape[0]
    assert taps - 1 <= HALO_ROWS and tm % HALO_ROWS == 0
    tiles_per_seq = seq // tm
    halo_blocks = tm // HALO_ROWS
    row = lambda i: (0, 0)
    in_specs = [
        pl.BlockSpec((tm, c), lambda i: (i, 0)),
        pl.BlockSpec((HALO_ROWS, c), lambda i: (jnp.maximum(i * halo_blocks - 1, 0), 0)),
        pl.BlockSpec((taps, c), row),
    ]
    scratch = [pltpu.VMEM((HALO_ROWS + tm, c), F32)]
    if mode == "ln_silu":
        in_specs += [pl.BlockSpec((1, c), row)] * 3
        args = (x, x, w) + tuple(v.reshape(1, c) for v in extra)
        scratch.append(pltpu.VMEM((tm, c), F32))
    else:
        in_specs.append(pl.BlockSpec((tm, c), lambda i: (i, 0)))
        args = (x, x, w) + tuple(extra)
    return pl.pallas_call(
        functools.partial(_conv_kernel, taps=taps, tm=tm, row_chunk=min(64, tm), lane_chunk=min(256, c),
                          tiles_per_seq=tiles_per_seq, mode=mode),
        grid=(t // tm,),
        in_specs=in_specs,
        out_specs=pl.BlockSpec((tm, c), lambda i: (i, 0)),
        out_shape=jax.ShapeDtypeStruct((t, c), BF16),
        scratch_shapes=scratch,
        compiler_params=_params(1),
        name=name,
    )(*args)


def _residual_ln(x, y, mod_ref, lng_ref, lnb_ref, alpha, gate_row):
    z = alpha * x + mod_ref[gate_row:gate_row + 1, :] * y
    return _layer_norm(z, lng_ref[...], lnb_ref[...])


def _top2_of4(a, b, c, d):
    hi1, lo1 = jnp.maximum(a, b), jnp.minimum(a, b)
    hi2, lo2 = jnp.maximum(c, d), jnp.minimum(c, d)
    return jnp.maximum(hi1, hi2) + jnp.maximum(jnp.minimum(hi1, hi2), jnp.maximum(lo1, lo2))


def _first_argmax(vals):
    idx = jnp.zeros(vals[0].shape, I32)
    best = vals[0]
    for j in range(1, len(vals)):
        take = vals[j] > best
        idx = jnp.where(take, j, idx)
        best = jnp.where(take, vals[j], best)
    return idx


def _select(rows, idx):
    out = rows[0]
    for j in range(1, len(rows)):
        out = jnp.where(idx == j, rows[j], out)
    return out


def _ln_router_kernel(x_ref, y_ref, mod_ref, lng_ref, lnb_ref, wrh_ref, wrl_ref, rb_ref, tri_ref,
                      xo_ref, h_ref, ri_ref, rw_ref, cnt_ref, carry_ref, *, alpha, n_experts):
    tm = x_ref.shape[0]
    per_group = n_experts // N_GROUPS

    @pl.when(pl.program_id(0) == 0)
    def _init():
        carry_ref[...] = jnp.zeros_like(carry_ref)

    xn = _residual_ln(x_ref[...], y_ref[...].astype(F32), mod_ref, lng_ref, lnb_ref, alpha, 2)
    xo_ref[...] = xn
    h = xn * (1.0 + mod_ref[4:5, :]) + mod_ref[3:4, :]
    h_hi = h.astype(BF16)
    h_ref[...] = h.reshape(h_ref.shape).astype(BF16)
    h_lo = (h - h_hi.astype(F32)).astype(BF16)
    logits = (jnp.dot(h_hi, wrh_ref[...], preferred_element_type=F32)
              + jnp.dot(h_lo, wrh_ref[...], preferred_element_type=F32)
              + jnp.dot(h_hi, wrl_ref[...], preferred_element_type=F32))
    lg = logits.T[0:n_experts, :]
    mx = jnp.max(lg, axis=0, keepdims=True)
    ex = jnp.exp(lg - mx)
    probs = ex / jnp.sum(ex, axis=0, keepdims=True)
    sel = probs + rb_ref[...]
    sel_rows = [sel[e:e + 1, :] for e in range(n_experts)]
    prob_rows = [probs[e:e + 1, :] for e in range(n_experts)]
    assert per_group == 4
    group_scores = [_top2_of4(*sel_rows[per_group * g:per_group * (g + 1)]) for g in range(N_GROUPS)]
    best_group = _first_argmax(group_scores)
    v = [_select([sel_rows[per_group * g + j] for g in range(N_GROUPS)], best_group) for j in range(per_group)]
    p = [_select([prob_rows[per_group * g + j] for g in range(N_GROUPS)], best_group) for j in range(per_group)]
    i1 = _first_argmax(v)
    i2 = _first_argmax([jnp.where(i1 == j, -jnp.inf, v[j]) for j in range(per_group)])
    p1, p2 = _select(p, i1), _select(p, i2)
    e1 = best_group * per_group + i1
    e2 = best_group * per_group + i2
    eid = lax.broadcasted_iota(I32, (n_experts, tm), 0)
    onehot = jnp.where((eid == e1) | (eid == e2), 1.0, 0.0)
    prefix = jnp.dot(onehot.astype(BF16), tri_ref[...], preferred_element_type=F32)
    tot = carry_ref[:, 0:1] + prefix
    rank1 = jnp.sum(jnp.where(eid == e1, tot, 0.0), axis=0, keepdims=True)
    rank2 = jnp.sum(jnp.where(eid == e2, tot, 0.0), axis=0, keepdims=True)
    new_carry = carry_ref[...] + jnp.sum(onehot, axis=1, keepdims=True)
    carry_ref[...] = new_carry
    cnt_ref[...] = new_carry.astype(I32)
    ri_ref[0:1, :] = e1
    ri_ref[1:2, :] = e2
    ri_ref[2:3, :] = rank1.astype(I32)
    ri_ref[3:4, :] = rank2.astype(I32)
    ri_ref[4:8, :] = jnp.zeros((4, tm), I32)
    rowid = lax.broadcasted_iota(I32, (LANES, tm), 0)
    wsum = p1 + p2
    slab = jnp.where(rowid == 0, p1 / wsum, jnp.where(rowid == 1, p2 / wsum, 0.0))
    rw_ref[...] = slab.T


def _ln_router(x, y, mod, layer, lng, lnb, w_router, router_bias, seq, alpha, tm):
    t, d = x.shape
    n_experts = w_router.shape[1]
    tiles_per_seq = seq // tm
    wr = jnp.zeros((d, LANES), F32).at[:, :n_experts].set(w_router)
    wr_hi = wr.astype(BF16)
    wr_lo = (wr - wr_hi.astype(F32)).astype(BF16)
    rb = jnp.broadcast_to(router_bias.astype(F32)[:, None], (n_experts, tm))
    tri = (lax.broadcasted_iota(I32, (tm, tm), 0) < lax.broadcasted_iota(I32, (tm, tm), 1)).astype(BF16)
    tok = lambda i: (i, 0)
    const = lambda i: (0, 0)
    return pl.pallas_call(
        functools.partial(_ln_router_kernel, alpha=alpha, n_experts=n_experts),
        grid=(t // tm,),
        in_specs=[
            pl.BlockSpec((tm, d), tok),
            pl.BlockSpec((tm, d), tok),
            pl.BlockSpec((None, None, N_MOD, d), lambda i: (layer, i // tiles_per_seq, 0, 0)),
            pl.BlockSpec((1, d), const),
            pl.BlockSpec((1, d), const),
            pl.BlockSpec((d, LANES), const),
            pl.BlockSpec((d, LANES), const),
            pl.BlockSpec((n_experts, tm), const),
            pl.BlockSpec((tm, tm), const),
        ],
        out_specs=[
            pl.BlockSpec((tm, d), tok),
            pl.BlockSpec((tm, d // LANES, LANES), lambda i: (i, 0, 0)),
            pl.BlockSpec((SUBLANES, tm), lambda i: (0, i)),
            pl.BlockSpec((tm, LANES), tok),
            pl.BlockSpec((n_experts, LANES), const),
        ],
        out_shape=[
            jax.ShapeDtypeStruct((t, d), F32),
            jax.ShapeDtypeStruct((t, d // LANES, LANES), BF16),
            jax.ShapeDtypeStruct((SUBLANES, t), I32),
            jax.ShapeDtypeStruct((t, LANES), F32),
            jax.ShapeDtypeStruct((n_experts, LANES), I32),
        ],
        scratch_shapes=[pltpu.VMEM((n_experts, LANES), F32)],
        compiler_params=_params(1),
        name="ln_router",
    )(x, y, mod, lng.reshape(1, d), lnb.reshape(1, d), wr_hi, wr_lo, rb, tri)


def _row_copy(src_ref, src_row, dst_ref, dst_row, sem):
    return pltpu.make_async_copy(src_ref.at[pl.ds(src_row, 1)], dst_ref.at[pl.ds(dst_row, 1)], sem)


def _dispatch_kernel(pos1_ref, pos2_ref, h_ref, xs_in_ref, xs_ref, sem, *, tm):
    del xs_in_ref
    base = pl.program_id(0) * tm

    def start(t, carry):
        _row_copy(h_ref, base + t, xs_ref, pos1_ref[base + t], sem).start()
        _row_copy(h_ref, base + t, xs_ref, pos2_ref[base + t], sem).start()
        return carry

    def wait(t, carry):
        _row_copy(h_ref, 0, xs_ref, 0, sem).wait()
        _row_copy(h_ref, 0, xs_ref, 0, sem).wait()
        return carry

    lax.fori_loop(0, tm, start, 0)
    lax.fori_loop(0, tm, wait, 0)


def _dispatch(h, pos1, pos2, n_rows, tm):
    t = h.shape[0]
    xs0 = jnp.zeros((n_rows,) + h.shape[1:], h.dtype)
    return pl.pallas_call(
        functools.partial(_dispatch_kernel, tm=tm),
        grid_spec=pltpu.PrefetchScalarGridSpec(
            num_scalar_prefetch=2,
            grid=(t // tm,),
            in_specs=[pl.BlockSpec(memory_space=pl.ANY), pl.BlockSpec(memory_space=pl.ANY)],
            out_specs=pl.BlockSpec(memory_space=pl.ANY),
            scratch_shapes=[pltpu.SemaphoreType.DMA(())],
        ),
        out_shape=jax.ShapeDtypeStruct(xs0.shape, h.dtype),
        input_output_aliases={3: 0},
        compiler_params=_params(1),
        name="moe_dispatch",
    )(pos1, pos2, h, xs0)


def _first_tile_of_expert(te_ref, j):
    return (j == 0) | (te_ref[j] != te_ref[jnp.maximum(j - 1, 0)])


def _moe_up_kernel(te_ref, xt_ref, nt_ref, x_ref, wg_ref, wu_ref, o_ref, wb_ref):
    del xt_ref
    j = pl.program_id(1)

    @pl.when(_first_tile_of_expert(te_ref, j))
    def _cast_weights():
        wb_ref[0] = wg_ref[...].astype(BF16)
        wb_ref[1] = wu_ref[...].astype(BF16)

    @pl.when(j < nt_ref[0])
    def _compute():
        x = x_ref[...].reshape(x_ref.shape[0], wb_ref.shape[1])
        gate = jnp.dot(x, wb_ref[0], preferred_element_type=F32)
        up = jnp.dot(x, wb_ref[1], preferred_element_type=F32)
        o_ref[...] = (_silu(gate) * up).astype(o_ref.dtype)

    @pl.when(j >= nt_ref[0])
    def _unused_tile():
        o_ref[...] = jnp.zeros_like(o_ref)


def _moe_up(xs, w_gate, w_up, layer, te, xt, nt, tm, bf):
    n_rows = xs.shape[0]
    d, f = w_gate.shape[2], w_gate.shape[3]
    n_tiles = n_rows // tm
    wspec = pl.BlockSpec((None, None, d, bf), lambda jf, j, te, xt, nt: (layer, te[j], 0, jf))
    return pl.pallas_call(
        _moe_up_kernel,
        grid_spec=pltpu.PrefetchScalarGridSpec(
            num_scalar_prefetch=3,
            grid=(f // bf, n_tiles),
            in_specs=[pl.BlockSpec((tm, d // LANES, LANES), lambda jf, j, te, xt, nt: (xt[j], 0, 0)),
                      wspec, wspec],
            out_specs=pl.BlockSpec((tm, bf), lambda jf, j, te, xt, nt: (j, jf)),
            scratch_shapes=[pltpu.VMEM((2, d, bf), BF16)],
        ),
        out_shape=jax.ShapeDtypeStruct((n_rows, f), BF16),
        compiler_params=_params(2),
        name="moe_up",
    )(te, xt, nt, xs, w_gate, w_up)


def _moe_down_kernel(te_ref, xt_ref, nt_ref, x_ref, w_ref, o_ref, wb_ref):
    del xt_ref
    j = pl.program_id(1)

    @pl.when(_first_tile_of_expert(te_ref, j))
    def _cast_weights():
        wb_ref[...] = w_ref[...].astype(BF16)

    @pl.when(j < nt_ref[0])
    def _compute():
        acc = jnp.dot(x_ref[...], wb_ref[...], preferred_element_type=F32)
        o_ref[...] = acc.reshape(o_ref.shape).astype(o_ref.dtype)

    @pl.when(j >= nt_ref[0])
    def _unused_tile():
        o_ref[...] = jnp.zeros_like(o_ref)


def _moe_down(hid, w_down, layer, te, xt, nt, tm, bn):
    n_rows, f = hid.shape
    d = w_down.shape[3]
    n_tiles = n_rows // tm
    return pl.pallas_call(
        _moe_down_kernel,
        grid_spec=pltpu.PrefetchScalarGridSpec(
            num_scalar_prefetch=3,
            grid=(d // bn, n_tiles),
            in_specs=[
                pl.BlockSpec((tm, f), lambda jn, j, te, xt, nt: (xt[j], 0)),
                pl.BlockSpec((None, None, f, bn), lambda jn, j, te, xt, nt: (layer, te[j], 0, jn)),
            ],
            out_specs=pl.BlockSpec((tm, bn // LANES, LANES), lambda jn, j, te, xt, nt: (j, jn, 0)),
            scratch_shapes=[pltpu.VMEM((f, bn), BF16)],
        ),
        out_shape=jax.ShapeDtypeStruct((n_rows, d // LANES, LANES), BF16),
        compiler_params=_params(2),
        name="moe_down",
    )(te, xt, nt, hid, w_down)


def _combine_ln_kernel(pos1_ref, pos2_ref, x_ref, rw_ref, mod_ref, lng_ref, lnb_ref, nmod_ref, ys_ref,
                       *rest, alpha, emit_h):
    if emit_h:
        xo_ref, h_ref, buf_ref, sem = rest
    else:
        xo_ref, buf_ref, sem = rest
    tm = x_ref.shape[0]
    base = pl.program_id(0) * tm

    def start(t, carry):
        _row_copy(ys_ref, pos1_ref[base + t], buf_ref.at[0], t, sem).start()
        _row_copy(ys_ref, pos2_ref[base + t], buf_ref.at[1], t, sem).start()
        return carry

    def wait(t, carry):
        _row_copy(ys_ref, 0, buf_ref.at[0], 0, sem).wait()
        _row_copy(ys_ref, 0, buf_ref.at[1], 0, sem).wait()
        return carry

    lax.fori_loop(0, tm, start, 0)
    lax.fori_loop(0, tm, wait, 0)
    rw = rw_ref[...]
    y1 = buf_ref[0].reshape(x_ref.shape).astype(F32)
    y2 = buf_ref[1].reshape(x_ref.shape).astype(F32)
    y = rw[:, 0:1] * y1 + rw[:, 1:2] * y2
    xn = _residual_ln(x_ref[...], y, mod_ref, lng_ref, lnb_ref, alpha, 5)
    xo_ref[...] = xn
    if emit_h:
        h_ref[...] = (xn * (1.0 + nmod_ref[1:2, :]) + nmod_ref[0:1, :]).astype(BF16)


def _combine_ln(x, ys, pos1, pos2, rw, mod, layer, next_layer, lng, lnb, seq, alpha, tm):
    t, d = x.shape
    tiles_per_seq = seq // tm
    emit_h = next_layer is not None
    tok = lambda i, p1, p2: (i, 0)
    const = lambda i, p1, p2: (0, 0)
    mod_spec = lambda l: pl.BlockSpec((None, None, N_MOD, d), lambda i, p1, p2: (l, i // tiles_per_seq, 0, 0))
    out_specs = [pl.BlockSpec((tm, d), tok)]
    out_shape = [jax.ShapeDtypeStruct((t, d), F32)]
    if emit_h:
        out_specs.append(pl.BlockSpec((tm, d), tok))
        out_shape.append(jax.ShapeDtypeStruct((t, d), BF16))
    outs = pl.pallas_call(
        functools.partial(_combine_ln_kernel, alpha=alpha, emit_h=emit_h),
        grid_spec=pltpu.PrefetchScalarGridSpec(
            num_scalar_prefetch=2,
            grid=(t // tm,),
            in_specs=[
                pl.BlockSpec((tm, d), tok),
                pl.BlockSpec((tm, LANES), tok),
                mod_spec(layer),
                pl.BlockSpec((1, d), const),
                pl.BlockSpec((1, d), const),
                mod_spec(next_layer if emit_h else layer),
                pl.BlockSpec(memory_space=pl.ANY),
            ],
            out_specs=out_specs,
            scratch_shapes=[pltpu.VMEM((2, tm, d // LANES, LANES), BF16), pltpu.SemaphoreType.DMA(())],
        ),
        out_shape=out_shape,
        compiler_params=_params(1),
        name="moe_combine_ln",
    )(pos1, pos2, x, rw, mod, lng.reshape(1, d), lnb.reshape(1, d), mod, ys)
    return (outs[0], outs[1]) if emit_h else (outs[0], None)


def _tiles(t, seq, d, d_ff):
    return dict(
        bm=min(1024, seq),
        bn1=min(512, d // 2),
        bn2=min(256, d // 2),
        mod_bn=min(1024, d),
        conv_tm=min(256, seq),
        ln_tm=min(256, seq),
        moe_tm=min(512, seq),
        moe_bf=min(512, d_ff),
        moe_bn=min(2048, d),
        dispatch_tm=min(256, seq),
    )


def _moe_schedule(counts, ranks, experts, tm, n_tiles):
    n_experts = counts.shape[0]
    tiles_per_expert = (counts + tm - 1) // tm
    tile_end = jnp.cumsum(tiles_per_expert)
    tile_start = tile_end - tiles_per_expert
    total = tile_end[-1]
    row_start = tile_start * tm
    pos = [row_start[experts[k]] + ranks[k] for k in range(2)]
    j = jnp.arange(n_tiles, dtype=I32)
    xt = jnp.minimum(j, total - 1)
    te = jnp.minimum(jnp.sum((xt[:, None] >= tile_end[None, :]).astype(I32), axis=1), n_experts - 1)
    return pos[0].astype(I32), pos[1].astype(I32), te.astype(I32), xt.astype(I32), total.reshape(1).astype(I32)


def kernel(x, c, w_ada, b_ada, w_in, conv_a_w, conv_a_b, ln_a_g, ln_a_b, conv_b_w, w_pa, w_pb, w_o,
           ln_mix_g, ln_mix_b, w_router, router_bias, w_gate, w_up, w_down, ln_ffn_g, ln_ffn_b):
    batch, seq, d = x.shape
    n_layers = w_in.shape[0]
    d_conf = conv_a_w.shape[2]
    d_short = conv_b_w.shape[2]
    n_experts, d_ff = w_gate.shape[1], w_gate.shape[3]
    t = batch * seq
    alpha = float((2 * n_layers) ** 0.25)
    ts = _tiles(t, seq, d, d_ff)
    bm, bn1, bn2 = ts["bm"], ts["bn1"], ts["bn2"]
    c_aval, c_agate = 0, d_conf
    c_sb, c_sc, c_sx = 2 * d_conf, 2 * d_conf + d_short, 2 * d_conf + 2 * d_short
    c_ga = 2 * d_conf + 3 * d_short
    n_moe_tiles = (2 * t) // ts["moe_tm"] + n_experts
    n_moe_rows = n_moe_tiles * ts["moe_tm"]

    mod = _modulation(c, w_ada, b_ada, ts["mod_bn"])
    xc = x.reshape(t, d)
    h = _modulate(xc, mod, 0, seq, ts["ln_tm"])
    for l in range(n_layers):
        w_in_l = lambda col0: (w_in, l, col0, 0)
        ya_pre = _matmul([h], [w_in_l(c_aval), w_in_l(c_agate)], [], _ep_glu, d_conf, F32, bm, bn2, "inproj_glu")
        cx = _matmul([h], [w_in_l(c_sc), w_in_l(c_sx)], [], _ep_mul, d_short, F32, bm, bn2, "inproj_cx")
        sb = _matmul([h], [w_in_l(c_sb)], [], _ep_id, d_short, BF16, bm, bn1, "inproj_sb")
        gates = _matmul([h], [w_in_l(c_ga)], [], _ep_sigmoid, 2 * d, BF16, bm, bn1, "inproj_gates")
        ya = _causal_conv(ya_pre, conv_a_w[l], seq, ts["conv_tm"], "ln_silu",
                          (conv_a_b[l], ln_a_g[l], ln_a_b[l]), "conv_a")
        yb = _causal_conv(cx, conv_b_w[l], seq, ts["conv_tm"], "gate", (sb,), "conv_b")
        m = _matmul([ya, yb], [(w_pa, l, 0, 0), (w_pb, l, 0, 1)], [(gates, 0), (gates, d)],
                    _ep_merge, d, BF16, bm, bn2, "proj_merge")
        y = _matmul([m], [(w_o, l, 0, 0)], [], _ep_id, d, BF16, bm, bn1, "out_proj")
        xc, h2, ri, rw, cnt = _ln_router(xc, y, mod, l, ln_mix_g[l], ln_mix_b[l], w_router, router_bias,
                                         seq, alpha, ts["ln_tm"])
        pos1, pos2, te, xt, nt = _moe_schedule(cnt[:, 0], (ri[2], ri[3]), (ri[0], ri[1]),
                                               ts["moe_tm"], n_moe_tiles)
        xs = _dispatch(h2, pos1, pos2, n_moe_rows, ts["dispatch_tm"])
        hid = _moe_up(xs, w_gate, w_up, l, te, xt, nt, ts["moe_tm"], ts["moe_bf"])
        ys = _moe_down(hid, w_down, l, te, xt, nt, ts["moe_tm"], ts["moe_bn"])
        xc, h = _combine_ln(xc, ys, pos1, pos2, rw, mod, l, l + 1 if l + 1 < n_layers else None,
                            ln_ffn_g[l], ln_ffn_b[l], seq, alpha, ts["ln_tm"])
    return xc.reshape(batch, seq, d)
```

```python
import functools

import jax
import jax.numpy as jnp
from jax import lax
from jax.experimental import pallas as pl
from jax.experimental.pallas import tpu as pltpu

F32 = jnp.float32
BF16 = jnp.bfloat16
I32 = jnp.int32

LN_EPS = 1e-5
N_GROUPS = 4
N_MOD = 6
V7X_VMEM_LIMIT_BYTES = 60000 * 1024
LANES = 128
SUBLANES = 8
HALO_ROWS = 32


def _params(n_grid):
    return pltpu.CompilerParams(
        dimension_semantics=("arbitrary",) * n_grid, vmem_limit_bytes=V7X_VMEM_LIMIT_BYTES)


def _sigmoid(v):
    return 1.0 / (1.0 + jnp.exp(-v))


def _silu(v):
    return v * _sigmoid(v)


def _layer_norm(z, g, b):
    mu = jnp.mean(z, axis=-1, keepdims=True)
    zc = z - mu
    var = jnp.mean(zc * zc, axis=-1, keepdims=True)
    return zc * lax.rsqrt(var + LN_EPS) * g + b


def _mod_kernel(c_ref, w_ref, b_ref, o_ref):
    cond = _silu(c_ref[...])
    acc = jnp.dot(cond.astype(BF16), w_ref[...].astype(BF16), preferred_element_type=F32)
    o_ref[...] = acc + b_ref[...]


def _modulation(c, w_ada, b_ada, bn):
    n_layers, d, n = w_ada.shape
    b = c.shape[0]
    c_pad = jnp.zeros((SUBLANES, d), F32).at[:b].set(c)
    out = pl.pallas_call(
        _mod_kernel,
        grid=(n_layers, n // bn),
        in_specs=[
            pl.BlockSpec((SUBLANES, d), lambda l, j: (0, 0)),
            pl.BlockSpec((None, d, bn), lambda l, j: (l, 0, j)),
            pl.BlockSpec((None, 1, bn), lambda l, j: (l, 0, j)),
        ],
        out_specs=pl.BlockSpec((None, SUBLANES, bn), lambda l, j: (l, 0, j)),
        out_shape=jax.ShapeDtypeStruct((n_layers, SUBLANES, n), F32),
        compiler_params=_params(2),
        name="adaln_mod",
    )(c_pad, w_ada, b_ada.reshape(n_layers, 1, n))
    return out[:, :b].reshape(n_layers, b, N_MOD, d)


def _modulate_kernel(x_ref, mod_ref, h_ref, *, shift_row, scale_row):
    sc = mod_ref[scale_row:scale_row + 1, :]
    sh = mod_ref[shift_row:shift_row + 1, :]
    h_ref[...] = (x_ref[...] * (1.0 + sc) + sh).astype(BF16)


def _modulate(x2, mod, layer, seq, tm):
    t, d = x2.shape
    tiles_per_seq = seq // tm
    return pl.pallas_call(
        functools.partial(_modulate_kernel, shift_row=0, scale_row=1),
        grid=(t // tm,),
        in_specs=[
            pl.BlockSpec((tm, d), lambda i: (i, 0)),
            pl.BlockSpec((None, None, N_MOD, d), lambda i: (layer, i // tiles_per_seq, 0, 0)),
        ],
        out_specs=pl.BlockSpec((tm, d), lambda i: (i, 0)),
        out_shape=jax.ShapeDtypeStruct((t, d), BF16),
        compiler_params=_params(1),
        name="modulate_in",
    )(x2, mod)


def _mm_kernel(*refs, n_x, w_x, n_e, epilogue):
    n_w = len(w_x)
    x_refs = refs[:n_x]
    w_refs = refs[n_x:n_x + n_w]
    e_refs = refs[n_x + n_w:n_x + n_w + n_e]
    o_ref, wb_ref = refs[-2], refs[-1]

    @pl.when(pl.program_id(1) == 0)
    def _cast_weights():
        for i in range(n_w):
            wb_ref[i] = w_refs[i][...].astype(BF16)

    accs = [jnp.dot(x_refs[w_x[i]][...], wb_ref[i], preferred_element_type=F32) for i in range(n_w)]
    o_ref[...] = epilogue(accs, [e[...] for e in e_refs]).astype(o_ref.dtype)


def _matmul(xs, ws, extras, epilogue, n_out, out_dtype, bm, bn, name):
    m, k = xs[0].shape
    n_w = len(ws)
    in_specs = [pl.BlockSpec((bm, x.shape[1]), lambda j, i: (i, 0)) for x in xs]
    args = list(xs)
    for w, lead, col0, _ in ws:
        assert col0 % bn == 0 and w.shape[1] == k
        in_specs.append(pl.BlockSpec((None, k, bn), functools.partial(
            lambda j, i, lead, cb: (lead, 0, cb + j), lead=lead, cb=col0 // bn)))
        args.append(w)
    for e, col0 in extras:
        assert col0 % bn == 0
        in_specs.append(pl.BlockSpec((bm, bn), functools.partial(
            lambda j, i, cb: (i, cb + j), cb=col0 // bn)))
        args.append(e)
    return pl.pallas_call(
        functools.partial(_mm_kernel, n_x=len(xs), w_x=tuple(w[3] for w in ws), n_e=len(extras),
                          epilogue=epilogue),
        grid=(n_out // bn, m // bm),
        in_specs=in_specs,
        out_specs=pl.BlockSpec((bm, bn), lambda j, i: (i, j)),
        out_shape=jax.ShapeDtypeStruct((m, n_out), out_dtype),
        scratch_shapes=[pltpu.VMEM((n_w, k, bn), BF16)],
        compiler_params=_params(2),
        name=name,
    )(*args)


def _ep_glu(accs, _):
    return accs[0] * _sigmoid(accs[1])


def _ep_mul(accs, _):
    return accs[0] * accs[1]


def _ep_id(accs, _):
    return accs[0]


def _ep_sigmoid(accs, _):
    return _sigmoid(accs[0])


def _ep_merge(accs, extras):
    return extras[0].astype(F32) * accs[0] + extras[1].astype(F32) * accs[1]


def _conv_shifts(taps):
    starts = [HALO_ROWS - (taps - 1) + k for k in range(taps)]
    return starts, sorted({s % SUBLANES for s in starts} - {0})


def _conv_kernel(*refs, taps, tm, row_chunk, lane_chunk, tiles_per_seq, mode):
    if mode == "ln_silu":
        cur_ref, halo_ref, w_ref, b_ref, g_ref, beta_ref, o_ref, buf_ref, sh_ref, y_ref = refs
    else:
        cur_ref, halo_ref, w_ref, gate_ref, o_ref, buf_ref, sh_ref = refs
    c = cur_ref.shape[1]
    first = (pl.program_id(0) % tiles_per_seq) == 0
    buf_ref[0:HALO_ROWS, :] = jnp.where(first, 0.0, halo_ref[...])
    buf_ref[HALO_ROWS:HALO_ROWS + tm, :] = cur_ref[...]
    starts, residues = _conv_shifts(taps)
    sh_rows = sh_ref.shape[1]
    for n, r in enumerate(residues):
        sh_ref[n] = buf_ref[r:r + sh_rows, :]
    for r0 in range(0, tm, row_chunk):
        for c0 in range(0, c, lane_chunk):
            acc = jnp.zeros((row_chunk, lane_chunk), F32)
            for kk in range(taps):
                q, r = divmod(starts[kk], SUBLANES)
                lo = q * SUBLANES + r0
                if r == 0:
                    xk = buf_ref[lo:lo + row_chunk, c0:c0 + lane_chunk]
                else:
                    xk = sh_ref[residues.index(r), lo:lo + row_chunk, c0:c0 + lane_chunk]
                acc = acc + w_ref[kk:kk + 1, c0:c0 + lane_chunk] * xk
            if mode == "ln_silu":
                y_ref[r0:r0 + row_chunk, c0:c0 + lane_chunk] = acc + b_ref[:, c0:c0 + lane_chunk]
            else:
                gate = gate_ref[r0:r0 + row_chunk, c0:c0 + lane_chunk].astype(F32)
                o_ref[r0:r0 + row_chunk, c0:c0 + lane_chunk] = (gate * acc).astype(o_ref.dtype)
    if mode == "ln_silu":
        o_ref[...] = _silu(_layer_norm(y_ref[...], g_ref[...], beta_ref[...])).astype(o_ref.dtype)


def _causal_conv(x, w, seq, tm, mode, extra, name):
    t, c = x.shape
    taps = w.shape[0]
    assert taps - 1 <= HALO_ROWS and tm % HALO_ROWS == 0
    tiles_per_seq = seq // tm
    halo_blocks = tm // HALO_ROWS
    row = lambda i: (0, 0)
    in_specs = [
        pl.BlockSpec((tm, c), lambda i: (i, 0)),
        pl.BlockSpec((HALO_ROWS, c), lambda i: (jnp.maximum(i * halo_blocks - 1, 0), 0)),
        pl.BlockSpec((taps, c), row),
    ]
    _, residues = _conv_shifts(taps)
    scratch = [pltpu.VMEM((HALO_ROWS + tm, c), F32),
               pltpu.VMEM((max(len(residues), 1), HALO_ROWS + tm - SUBLANES, c), F32)]
    if mode == "ln_silu":
        in_specs += [pl.BlockSpec((1, c), row)] * 3
        args = (x, x, w) + tuple(v.reshape(1, c) for v in extra)
        scratch.append(pltpu.VMEM((tm, c), F32))
    else:
        in_specs.append(pl.BlockSpec((tm, c), lambda i: (i, 0)))
        args = (x, x, w) + tuple(extra)
    return pl.pallas_call(
        functools.partial(_conv_kernel, taps=taps, tm=tm, row_chunk=min(64, tm), lane_chunk=min(256, c),
                          tiles_per_seq=tiles_per_seq, mode=mode),
        grid=(t // tm,),
        in_specs=in_specs,
        out_specs=pl.BlockSpec((tm, c), lambda i: (i, 0)),
        out_shape=jax.ShapeDtypeStruct((t, c), BF16),
        scratch_shapes=scratch,
        compiler_params=_params(1),
        name=name,
    )(*args)


def _residual_ln(x, y, mod_ref, lng_ref, lnb_ref, alpha, gate_row):
    z = alpha * x + mod_ref[gate_row:gate_row + 1, :] * y
    return _layer_norm(z, lng_ref[...], lnb_ref[...])


def _top2_of4(a, b, c, d):
    hi1, lo1 = jnp.maximum(a, b), jnp.minimum(a, b)
    hi2, lo2 = jnp.maximum(c, d), jnp.minimum(c, d)
    return jnp.maximum(hi1, hi2) + jnp.maximum(jnp.minimum(hi1, hi2), jnp.maximum(lo1, lo2))


def _first_argmax(vals):
    idx = jnp.zeros(vals[0].shape, I32)
    best = vals[0]
    for j in range(1, len(vals)):
        take = vals[j] > best
        idx = jnp.where(take, j, idx)
        best = jnp.where(take, vals[j], best)
    return idx


def _select(rows, idx):
    out = rows[0]
    for j in range(1, len(rows)):
        out = jnp.where(idx == j, rows[j], out)
    return out


def _ln_router_kernel(x_ref, y_ref, mod_ref, lng_ref, lnb_ref, wrh_ref, wrl_ref, rb_ref, tri_ref,
                      xo_ref, h_ref, ri_ref, rw_ref, cnt_ref, carry_ref, *, alpha, n_experts):
    tm = x_ref.shape[0]
    per_group = n_experts // N_GROUPS

    @pl.when(pl.program_id(0) == 0)
    def _init():
        carry_ref[...] = jnp.zeros_like(carry_ref)

    xn = _residual_ln(x_ref[...], y_ref[...].astype(F32), mod_ref, lng_ref, lnb_ref, alpha, 2)
    xo_ref[...] = xn
    h = xn * (1.0 + mod_ref[4:5, :]) + mod_ref[3:4, :]
    h_hi = h.astype(BF16)
    h_ref[...] = h.reshape(h_ref.shape).astype(BF16)
    h_lo = (h - h_hi.astype(F32)).astype(BF16)
    logits = (jnp.dot(h_hi, wrh_ref[...], preferred_element_type=F32)
              + jnp.dot(h_lo, wrh_ref[...], preferred_element_type=F32)
              + jnp.dot(h_hi, wrl_ref[...], preferred_element_type=F32))
    lg = logits.T[0:n_experts, :]
    mx = jnp.max(lg, axis=0, keepdims=True)
    ex = jnp.exp(lg - mx)
    probs = ex / jnp.sum(ex, axis=0, keepdims=True)
    sel = probs + rb_ref[...]
    sel_rows = [sel[e:e + 1, :] for e in range(n_experts)]
    prob_rows = [probs[e:e + 1, :] for e in range(n_experts)]
    assert per_group == 4
    group_scores = [_top2_of4(*sel_rows[per_group * g:per_group * (g + 1)]) for g in range(N_GROUPS)]
    best_group = _first_argmax(group_scores)
    v = [_select([sel_rows[per_group * g + j] for g in range(N_GROUPS)], best_group) for j in range(per_group)]
    p = [_select([prob_rows[per_group * g + j] for g in range(N_GROUPS)], best_group) for j in range(per_group)]
    i1 = _first_argmax(v)
    i2 = _first_argmax([jnp.where(i1 == j, -jnp.inf, v[j]) for j in range(per_group)])
    p1, p2 = _select(p, i1), _select(p, i2)
    e1 = best_group * per_group + i1
    e2 = best_group * per_group + i2
    eid = lax.broadcasted_iota(I32, (n_experts, tm), 0)
    onehot = jnp.where((eid == e1) | (eid == e2), 1.0, 0.0)
    prefix = jnp.dot(onehot.astype(BF16), tri_ref[...], preferred_element_type=F32)
    tot = carry_ref[:, 0:1] + prefix
    rank1 = jnp.sum(jnp.where(eid == e1, tot, 0.0), axis=0, keepdims=True)
    rank2 = jnp.sum(jnp.where(eid == e2, tot, 0.0), axis=0, keepdims=True)
    new_carry = carry_ref[...] + jnp.sum(onehot, axis=1, keepdims=True)
    carry_ref[...] = new_carry
    cnt_ref[...] = new_carry.astype(I32)
    ri_ref[0:1, :] = e1
    ri_ref[1:2, :] = e2
    ri_ref[2:3, :] = rank1.astype(I32)
    ri_ref[3:4, :] = rank2.astype(I32)
    ri_ref[4:8, :] = jnp.zeros((4, tm), I32)
    rowid = lax.broadcasted_iota(I32, (LANES, tm), 0)
    wsum = p1 + p2
    slab = jnp.where(rowid == 0, p1 / wsum, jnp.where(rowid == 1, p2 / wsum, 0.0))
    rw_ref[...] = slab.T


def _ln_router(x, y, mod, layer, lng, lnb, w_router, router_bias, seq, alpha, tm):
    t, d = x.shape
    n_experts = w_router.shape[1]
    tiles_per_seq = seq // tm
    wr = jnp.zeros((d, LANES), F32).at[:, :n_experts].set(w_router)
    wr_hi = wr.astype(BF16)
    wr_lo = (wr - wr_hi.astype(F32)).astype(BF16)
    rb = jnp.broadcast_to(router_bias.astype(F32)[:, None], (n_experts, tm))
    tri = (lax.broadcasted_iota(I32, (tm, tm), 0) < lax.broadcasted_iota(I32, (tm, tm), 1)).astype(BF16)
    tok = lambda i: (i, 0)
    const = lambda i: (0, 0)
    return pl.pallas_call(
        functools.partial(_ln_router_kernel, alpha=alpha, n_experts=n_experts),
        grid=(t // tm,),
        in_specs=[
            pl.BlockSpec((tm, d), tok),
            pl.BlockSpec((tm, d), tok),
            pl.BlockSpec((None, None, N_MOD, d), lambda i: (layer, i // tiles_per_seq, 0, 0)),
            pl.BlockSpec((1, d), const),
            pl.BlockSpec((1, d), const),
            pl.BlockSpec((d, LANES), const),
            pl.BlockSpec((d, LANES), const),
            pl.BlockSpec((n_experts, tm), const),
            pl.BlockSpec((tm, tm), const),
        ],
        out_specs=[
            pl.BlockSpec((tm, d), tok),
            pl.BlockSpec((tm, d // LANES, LANES), lambda i: (i, 0, 0)),
            pl.BlockSpec((SUBLANES, tm), lambda i: (0, i)),
            pl.BlockSpec((tm, LANES), tok),
            pl.BlockSpec((n_experts, LANES), const),
        ],
        out_shape=[
            jax.ShapeDtypeStruct((t, d), F32),
            jax.ShapeDtypeStruct((t, d // LANES, LANES), BF16),
            jax.ShapeDtypeStruct((SUBLANES, t), I32),
            jax.ShapeDtypeStruct((t, LANES), F32),
            jax.ShapeDtypeStruct((n_experts, LANES), I32),
        ],
        scratch_shapes=[pltpu.VMEM((n_experts, LANES), F32)],
        compiler_params=_params(1),
        name="ln_router",
    )(x, y, mod, lng.reshape(1, d), lnb.reshape(1, d), wr_hi, wr_lo, rb, tri)


def _row_copy(src_ref, src_row, dst_ref, dst_row, sem):
    return pltpu.make_async_copy(src_ref.at[pl.ds(src_row, 1)], dst_ref.at[pl.ds(dst_row, 1)], sem)


def _dispatch_kernel(pos1_ref, pos2_ref, h_ref, xs_in_ref, xs_ref, sem, *, tm):
    del xs_in_ref
    base = pl.program_id(0) * tm

    def start(t, carry):
        _row_copy(h_ref, t, xs_ref, pos1_ref[base + t], sem).start()
        _row_copy(h_ref, t, xs_ref, pos2_ref[base + t], sem).start()
        return carry

    lax.fori_loop(0, tm, start, 0, unroll=8)
    for _ in range(2):
        pltpu.make_async_copy(h_ref, xs_ref.at[pl.ds(0, tm)], sem).wait()


def _dispatch(h, pos1, pos2, n_rows, tm):
    t = h.shape[0]
    xs0 = jnp.zeros((n_rows,) + h.shape[1:], h.dtype)
    return pl.pallas_call(
        functools.partial(_dispatch_kernel, tm=tm),
        grid_spec=pltpu.PrefetchScalarGridSpec(
            num_scalar_prefetch=2,
            grid=(t // tm,),
            in_specs=[pl.BlockSpec((tm,) + h.shape[1:], lambda i, p1, p2: (i, 0, 0)),
                      pl.BlockSpec(memory_space=pl.ANY)],
            out_specs=pl.BlockSpec(memory_space=pl.ANY),
            scratch_shapes=[pltpu.SemaphoreType.DMA(())],
        ),
        out_shape=jax.ShapeDtypeStruct(xs0.shape, h.dtype),
        input_output_aliases={3: 0},
        compiler_params=_params(1),
        name="moe_dispatch",
    )(pos1, pos2, h, xs0)


def _first_tile_of_expert(te_ref, j):
    return (j == 0) | (te_ref[j] != te_ref[jnp.maximum(j - 1, 0)])


def _moe_up_kernel(te_ref, xt_ref, nt_ref, x_ref, wg_ref, wu_ref, o_ref, wb_ref):
    del xt_ref
    j = pl.program_id(1)

    @pl.when(_first_tile_of_expert(te_ref, j))
    def _cast_weights():
        wb_ref[0] = wg_ref[...].astype(BF16)
        wb_ref[1] = wu_ref[...].astype(BF16)

    @pl.when(j < nt_ref[0])
    def _compute():
        x = x_ref[...].reshape(x_ref.shape[0], wb_ref.shape[1])
        gate = jnp.dot(x, wb_ref[0], preferred_element_type=F32)
        up = jnp.dot(x, wb_ref[1], preferred_element_type=F32)
        o_ref[...] = (_silu(gate) * up).astype(o_ref.dtype)

    @pl.when(j >= nt_ref[0])
    def _unused_tile():
        o_ref[...] = jnp.zeros_like(o_ref)


def _moe_up(xs, w_gate, w_up, layer, te, xt, nt, tm, bf):
    n_rows = xs.shape[0]
    d, f = w_gate.shape[2], w_gate.shape[3]
    n_tiles = n_rows // tm
    wspec = pl.BlockSpec((None, None, d, bf), lambda jf, j, te, xt, nt: (layer, te[j], 0, jf))
    return pl.pallas_call(
        _moe_up_kernel,
        grid_spec=pltpu.PrefetchScalarGridSpec(
            num_scalar_prefetch=3,
            grid=(f // bf, n_tiles),
            in_specs=[pl.BlockSpec((tm, d // LANES, LANES), lambda jf, j, te, xt, nt: (xt[j], 0, 0)),
                      wspec, wspec],
            out_specs=pl.BlockSpec((tm, bf), lambda jf, j, te, xt, nt: (j, jf)),
            scratch_shapes=[pltpu.VMEM((2, d, bf), BF16)],
        ),
        out_shape=jax.ShapeDtypeStruct((n_rows, f), BF16),
        compiler_params=_params(2),
        name="moe_up",
    )(te, xt, nt, xs, w_gate, w_up)


def _moe_down_kernel(te_ref, xt_ref, nt_ref, x_ref, w_ref, o_ref, wb_ref):
    del xt_ref
    j = pl.program_id(1)

    @pl.when(_first_tile_of_expert(te_ref, j))
    def _cast_weights():
        wb_ref[...] = w_ref[...].astype(BF16)

    @pl.when(j < nt_ref[0])
    def _compute():
        acc = jnp.dot(x_ref[...], wb_ref[...], preferred_element_type=F32)
        o_ref[...] = acc.reshape(o_ref.shape).astype(o_ref.dtype)

    @pl.when(j >= nt_ref[0])
    def _unused_tile():
        o_ref[...] = jnp.zeros_like(o_ref)


def _moe_down(hid, w_down, layer, te, xt, nt, tm, bn):
    n_rows, f = hid.shape
    d = w_down.shape[3]
    n_tiles = n_rows // tm
    return pl.pallas_call(
        _moe_down_kernel,
        grid_spec=pltpu.PrefetchScalarGridSpec(
            num_scalar_prefetch=3,
            grid=(d // bn, n_tiles),
            in_specs=[
                pl.BlockSpec((tm, f), lambda jn, j, te, xt, nt: (xt[j], 0)),
                pl.BlockSpec((None, None, f, bn), lambda jn, j, te, xt, nt: (layer, te[j], 0, jn)),
            ],
            out_specs=pl.BlockSpec((tm, bn // LANES, LANES), lambda jn, j, te, xt, nt: (j, jn, 0)),
            scratch_shapes=[pltpu.VMEM((f, bn), BF16)],
        ),
        out_shape=jax.ShapeDtypeStruct((n_rows, d // LANES, LANES), BF16),
        compiler_params=_params(2),
        name="moe_down",
    )(te, xt, nt, hid, w_down)


def _combine_ln_kernel(pos1_ref, pos2_ref, x_ref, rw_ref, mod_ref, lng_ref, lnb_ref, nmod_ref, ys_ref,
                       *rest, alpha, emit_h):
    if emit_h:
        xo_ref, h_ref, buf_ref, sem = rest
    else:
        xo_ref, buf_ref, sem = rest
    tm = x_ref.shape[0]
    i = pl.program_id(0)

    def gather(tile, slot):
        base = tile * tm

        def start(t, carry):
            _row_copy(ys_ref, pos1_ref[base + t], buf_ref.at[slot, 0], t, sem.at[slot]).start()
            _row_copy(ys_ref, pos2_ref[base + t], buf_ref.at[slot, 1], t, sem.at[slot]).start()
            return carry

        lax.fori_loop(0, tm, start, 0, unroll=8)

    @pl.when(i == 0)
    def _first_tile():
        gather(0, 0)

    @pl.when(i + 1 < pl.num_programs(0))
    def _prefetch():
        gather(i + 1, (i + 1) % 2)

    slot = i % 2
    for k in range(2):
        pltpu.make_async_copy(ys_ref.at[pl.ds(0, tm)], buf_ref.at[slot, k], sem.at[slot]).wait()
    rw = rw_ref[...]
    y1 = buf_ref[slot, 0].reshape(x_ref.shape).astype(F32)
    y2 = buf_ref[slot, 1].reshape(x_ref.shape).astype(F32)
    y = rw[:, 0:1] * y1 + rw[:, 1:2] * y2
    xn = _residual_ln(x_ref[...], y, mod_ref, lng_ref, lnb_ref, alpha, 5)
    xo_ref[...] = xn
    if emit_h:
        h_ref[...] = (xn * (1.0 + nmod_ref[1:2, :]) + nmod_ref[0:1, :]).astype(BF16)


def _combine_ln(x, ys, pos1, pos2, rw, mod, layer, next_layer, lng, lnb, seq, alpha, tm):
    t, d = x.shape
    tiles_per_seq = seq // tm
    emit_h = next_layer is not None
    tok = lambda i, p1, p2: (i, 0)
    const = lambda i, p1, p2: (0, 0)
    mod_spec = lambda l: pl.BlockSpec((None, None, N_MOD, d), lambda i, p1, p2: (l, i // tiles_per_seq, 0, 0))
    out_specs = [pl.BlockSpec((tm, d), tok)]
    out_shape = [jax.ShapeDtypeStruct((t, d), F32)]
    if emit_h:
        out_specs.append(pl.BlockSpec((tm, d), tok))
        out_shape.append(jax.ShapeDtypeStruct((t, d), BF16))
    outs = pl.pallas_call(
        functools.partial(_combine_ln_kernel, alpha=alpha, emit_h=emit_h),
        grid_spec=pltpu.PrefetchScalarGridSpec(
            num_scalar_prefetch=2,
            grid=(t // tm,),
            in_specs=[
                pl.BlockSpec((tm, d), tok),
                pl.BlockSpec((tm, LANES), tok),
                mod_spec(layer),
                pl.BlockSpec((1, d), const),
                pl.BlockSpec((1, d), const),
                mod_spec(next_layer if emit_h else layer),
                pl.BlockSpec(memory_space=pl.ANY),
            ],
            out_specs=out_specs,
            scratch_shapes=[pltpu.VMEM((2, 2, tm, d // LANES, LANES), BF16), pltpu.SemaphoreType.DMA((2,))],
        ),
        out_shape=out_shape,
        compiler_params=_params(1),
        name="moe_combine_ln",
    )(pos1, pos2, x, rw, mod, lng.reshape(1, d), lnb.reshape(1, d), mod, ys)
    return (outs[0], outs[1]) if emit_h else (outs[0], None)


def _tiles(t, seq, d, d_ff):
    return dict(
        bm=min(1024, seq),
        bn1=min(512, d // 2),
        bn2=min(256, d // 2),
        mod_bn=min(1024, d),
        conv_tm=min(256, seq),
        ln_tm=min(256, seq),
        moe_tm=min(512, seq),
        moe_bf=min(512, d_ff),
        moe_bn=min(4096, d),
        dispatch_tm=min(512, seq),
    )


def _moe_schedule(counts, ranks, experts, tm, n_tiles):
    n_experts = counts.shape[0]
    tiles_per_expert = (counts + tm - 1) // tm
    tile_end = jnp.cumsum(tiles_per_expert)
    tile_start = tile_end - tiles_per_expert
    total = tile_end[-1]
    row_start = tile_start * tm
    pos = [row_start[experts[k]] + ranks[k] for k in range(2)]
    j = jnp.arange(n_tiles, dtype=I32)
    xt = jnp.minimum(j, total - 1)
    te = jnp.minimum(jnp.sum((xt[:, None] >= tile_end[None, :]).astype(I32), axis=1), n_experts - 1)
    return pos[0].astype(I32), pos[1].astype(I32), te.astype(I32), xt.astype(I32), total.reshape(1).astype(I32)


def kernel(x, c, w_ada, b_ada, w_in, conv_a_w, conv_a_b, ln_a_g, ln_a_b, conv_b_w, w_pa, w_pb, w_o,
           ln_mix_g, ln_mix_b, w_router, router_bias, w_gate, w_up, w_down, ln_ffn_g, ln_ffn_b):
    batch, seq, d = x.shape
    n_layers = w_in.shape[0]
    d_conf = conv_a_w.shape[2]
    d_short = conv_b_w.shape[2]
    n_experts, d_ff = w_gate.shape[1], w_gate.shape[3]
    t = batch * seq
    alpha = float((2 * n_layers) ** 0.25)
    ts = _tiles(t, seq, d, d_ff)
    bm, bn1, bn2 = ts["bm"], ts["bn1"], ts["bn2"]
    c_aval, c_agate = 0, d_conf
    c_sb, c_sc, c_sx = 2 * d_conf, 2 * d_conf + d_short, 2 * d_conf + 2 * d_short
    c_ga = 2 * d_conf + 3 * d_short
    n_moe_tiles = (2 * t) // ts["moe_tm"] + n_experts
    n_moe_rows = n_moe_tiles * ts["moe_tm"]

    mod = _modulation(c, w_ada, b_ada, ts["mod_bn"])
    xc = x.reshape(t, d)
    h = _modulate(xc, mod, 0, seq, ts["ln_tm"])
    for l in range(n_layers):
        w_in_l = lambda col0: (w_in, l, col0, 0)
        ya_pre = _matmul([h], [w_in_l(c_aval), w_in_l(c_agate)], [], _ep_glu, d_conf, F32, bm, bn2, "inproj_glu")
        cx = _matmul([h], [w_in_l(c_sc), w_in_l(c_sx)], [], _ep_mul, d_short, F32, bm, bn2, "inproj_cx")
        sb = _matmul([h], [w_in_l(c_sb)], [], _ep_id, d_short, BF16, bm, bn1, "inproj_sb")
        gates = _matmul([h], [w_in_l(c_ga)], [], _ep_sigmoid, 2 * d, BF16, bm, bn1, "inproj_gates")
        ya = _causal_conv(ya_pre, conv_a_w[l], seq, ts["conv_tm"], "ln_silu",
                          (conv_a_b[l], ln_a_g[l], ln_a_b[l]), "conv_a")
        yb = _causal_conv(cx, conv_b_w[l], seq, ts["conv_tm"], "gate", (sb,), "conv_b")
        m = _matmul([ya, yb], [(w_pa, l, 0, 0), (w_pb, l, 0, 1)], [(gates, 0), (gates, d)],
                    _ep_merge, d, BF16, bm, bn1, "proj_merge")
        y = _matmul([m], [(w_o, l, 0, 0)], [], _ep_id, d, BF16, bm, bn1, "out_proj")
        xc, h2, ri, rw, cnt = _ln_router(xc, y, mod, l, ln_mix_g[l], ln_mix_b[l], w_router, router_bias,
                                         seq, alpha, ts["ln_tm"])
        pos1, pos2, te, xt, nt = _moe_schedule(cnt[:, 0], (ri[2], ri[3]), (ri[0], ri[1]),
                                               ts["moe_tm"], n_moe_tiles)
        xs = _dispatch(h2, pos1, pos2, n_moe_rows, ts["dispatch_tm"])
        hid = _moe_up(xs, w_gate, w_up, l, te, xt, nt, ts["moe_tm"], ts["moe_bf"])
        ys = _moe_down(hid, w_down, l, te, xt, nt, ts["moe_tm"], ts["moe_bn"])
        xc, h = _combine_ln(xc, ys, pos1, pos2, rw, mod, l, l + 1 if l + 1 < n_layers else None,
                            ln_ffn_g[l], ln_ffn_b[l], seq, alpha, ts["ln_tm"])
    return xc.reshape(batch, seq, d)
```

```python
import functools

import jax
import jax.numpy as jnp
from jax import lax
from jax.experimental import pallas as pl
from jax.experimental.pallas import tpu as pltpu

F32 = jnp.float32
BF16 = jnp.bfloat16
I32 = jnp.int32

LN_EPS = 1e-5
N_GROUPS = 4
N_MOD = 6
V7X_VMEM_LIMIT_BYTES = 60000 * 1024
LANES = 128
SUBLANES = 8
HALO_ROWS = 32
MM_ROW_BLOCK = 256


def _params(n_grid):
    return pltpu.CompilerParams(
        dimension_semantics=("arbitrary",) * n_grid, vmem_limit_bytes=V7X_VMEM_LIMIT_BYTES)


def _sigmoid(v):
    return 1.0 / (1.0 + jnp.exp(-v))


def _silu(v):
    return v * _sigmoid(v)


def _layer_norm(z, g, b):
    mu = jnp.mean(z, axis=-1, keepdims=True)
    zc = z - mu
    var = jnp.mean(zc * zc, axis=-1, keepdims=True)
    return zc * lax.rsqrt(var + LN_EPS) * g + b


def _mod_kernel(c_ref, w_ref, b_ref, o_ref):
    cond = _silu(c_ref[...])
    acc = jnp.dot(cond.astype(BF16), w_ref[...].astype(BF16), preferred_element_type=F32)
    o_ref[...] = acc + b_ref[...]


def _modulation(c, w_ada, b_ada, bn):
    n_layers, d, n = w_ada.shape
    b = c.shape[0]
    c_pad = jnp.zeros((SUBLANES, d), F32).at[:b].set(c)
    out = pl.pallas_call(
        _mod_kernel,
        grid=(n_layers, n // bn),
        in_specs=[
            pl.BlockSpec((SUBLANES, d), lambda l, j: (0, 0)),
            pl.BlockSpec((None, d, bn), lambda l, j: (l, 0, j)),
            pl.BlockSpec((None, 1, bn), lambda l, j: (l, 0, j)),
        ],
        out_specs=pl.BlockSpec((None, SUBLANES, bn), lambda l, j: (l, 0, j)),
        out_shape=jax.ShapeDtypeStruct((n_layers, SUBLANES, n), F32),
        compiler_params=_params(2),
        name="adaln_mod",
    )(c_pad, w_ada, b_ada.reshape(n_layers, 1, n))
    return out[:, :b].reshape(n_layers, b, N_MOD, d)


def _modulate_kernel(x_ref, mod_ref, h_ref, *, shift_row, scale_row):
    sc = mod_ref[scale_row:scale_row + 1, :]
    sh = mod_ref[shift_row:shift_row + 1, :]
    h_ref[...] = (x_ref[...] * (1.0 + sc) + sh).astype(BF16)


def _modulate(x2, mod, layer, seq, tm):
    t, d = x2.shape
    tiles_per_seq = seq // tm
    return pl.pallas_call(
        functools.partial(_modulate_kernel, shift_row=0, scale_row=1),
        grid=(t // tm,),
        in_specs=[
            pl.BlockSpec((tm, d), lambda i: (i, 0)),
            pl.BlockSpec((None, None, N_MOD, d), lambda i: (layer, i // tiles_per_seq, 0, 0)),
        ],
        out_specs=pl.BlockSpec((tm, d), lambda i: (i, 0)),
        out_shape=jax.ShapeDtypeStruct((t, d), BF16),
        compiler_params=_params(1),
        name="modulate_in",
    )(x2, mod)


def _mm_kernel(*refs, n_x, w_x, n_e, epilogue):
    n_w = len(w_x)
    x_refs = refs[:n_x]
    w_refs = refs[n_x:n_x + n_w]
    e_refs = refs[n_x + n_w:n_x + n_w + n_e]
    o_ref, wb_ref = refs[-2], refs[-1]

    @pl.when(pl.program_id(1) == 0)
    def _cast_weights():
        for i in range(n_w):
            wb_ref[i] = w_refs[i][...].astype(BF16)

    rows = min(MM_ROW_BLOCK, o_ref.shape[0])
    for r0 in range(0, o_ref.shape[0], rows):
        accs = [jnp.dot(x_refs[w_x[i]][r0:r0 + rows, :], wb_ref[i], preferred_element_type=F32)
                for i in range(n_w)]
        extras = [e[r0:r0 + rows, :] for e in e_refs]
        o_ref[r0:r0 + rows, :] = epilogue(accs, extras).astype(o_ref.dtype)


def _matmul(xs, ws, extras, epilogue, n_out, out_dtype, bm, bn, name):
    m, k = xs[0].shape
    n_w = len(ws)
    in_specs = [pl.BlockSpec((bm, x.shape[1]), lambda j, i: (i, 0)) for x in xs]
    args = list(xs)
    for w, lead, col0, _ in ws:
        assert col0 % bn == 0 and w.shape[1] == k
        in_specs.append(pl.BlockSpec((None, k, bn), functools.partial(
            lambda j, i, lead, cb: (lead, 0, cb + j), lead=lead, cb=col0 // bn)))
        args.append(w)
    for e, col0 in extras:
        assert col0 % bn == 0
        in_specs.append(pl.BlockSpec((bm, bn), functools.partial(
            lambda j, i, cb: (i, cb + j), cb=col0 // bn)))
        args.append(e)
    return pl.pallas_call(
        functools.partial(_mm_kernel, n_x=len(xs), w_x=tuple(w[3] for w in ws), n_e=len(extras),
                          epilogue=epilogue),
        grid=(n_out // bn, m // bm),
        in_specs=in_specs,
        out_specs=pl.BlockSpec((bm, bn), lambda j, i: (i, j)),
        out_shape=jax.ShapeDtypeStruct((m, n_out), out_dtype),
        scratch_shapes=[pltpu.VMEM((n_w, k, bn), BF16)],
        compiler_params=_params(2),
        name=name,
    )(*args)


def _ep_glu(accs, _):
    return accs[0] * _sigmoid(accs[1])


def _ep_mul(accs, _):
    return accs[0] * accs[1]


def _ep_id(accs, _):
    return accs[0]


def _ep_sigmoid(accs, _):
    return _sigmoid(accs[0])


def _ep_merge(accs, extras):
    return extras[0].astype(F32) * accs[0] + extras[1].astype(F32) * accs[1]


def _conv_shifts(taps):
    starts = [HALO_ROWS - (taps - 1) + k for k in range(taps)]
    return starts, sorted({s % SUBLANES for s in starts} - {0})


def _conv_kernel(*refs, taps, tm, row_chunk, lane_chunk, tiles_per_seq, mode):
    if mode == "ln_silu":
        cur_ref, halo_ref, w_ref, b_ref, g_ref, beta_ref, o_ref, buf_ref, sh_ref, y_ref = refs
    else:
        cur_ref, halo_ref, w_ref, gate_ref, o_ref, buf_ref, sh_ref = refs
    c = cur_ref.shape[1]
    first = (pl.program_id(0) % tiles_per_seq) == 0
    buf_ref[0:HALO_ROWS, :] = jnp.where(first, 0.0, halo_ref[...])
    buf_ref[HALO_ROWS:HALO_ROWS + tm, :] = cur_ref[...]
    starts, residues = _conv_shifts(taps)
    sh_rows = sh_ref.shape[1]
    for n, r in enumerate(residues):
        sh_ref[n] = buf_ref[r:r + sh_rows, :]
    for r0 in range(0, tm, row_chunk):
        for c0 in range(0, c, lane_chunk):
            acc = jnp.zeros((row_chunk, lane_chunk), F32)
            for kk in range(taps):
                q, r = divmod(starts[kk], SUBLANES)
                lo = q * SUBLANES + r0
                if r == 0:
                    xk = buf_ref[lo:lo + row_chunk, c0:c0 + lane_chunk]
                else:
                    xk = sh_ref[residues.index(r), lo:lo + row_chunk, c0:c0 + lane_chunk]
                acc = acc + w_ref[kk:kk + 1, c0:c0 + lane_chunk] * xk
            if mode == "ln_silu":
                y_ref[r0:r0 + row_chunk, c0:c0 + lane_chunk] = acc + b_ref[:, c0:c0 + lane_chunk]
            else:
                gate = gate_ref[r0:r0 + row_chunk, c0:c0 + lane_chunk].astype(F32)
                o_ref[r0:r0 + row_chunk, c0:c0 + lane_chunk] = (gate * acc).astype(o_ref.dtype)
    if mode == "ln_silu":
        o_ref[...] = _silu(_layer_norm(y_ref[...], g_ref[...], beta_ref[...])).astype(o_ref.dtype)


def _causal_conv(x, w, seq, tm, mode, extra, name):
    t, c = x.shape
    taps = w.shape[0]
    assert taps - 1 <= HALO_ROWS and tm % HALO_ROWS == 0
    tiles_per_seq = seq // tm
    halo_blocks = tm // HALO_ROWS
    row = lambda i: (0, 0)
    in_specs = [
        pl.BlockSpec((tm, c), lambda i: (i, 0)),
        pl.BlockSpec((HALO_ROWS, c), lambda i: (jnp.maximum(i * halo_blocks - 1, 0), 0)),
        pl.BlockSpec((taps, c), row),
    ]
    _, residues = _conv_shifts(taps)
    scratch = [pltpu.VMEM((HALO_ROWS + tm, c), F32),
               pltpu.VMEM((max(len(residues), 1), HALO_ROWS + tm - SUBLANES, c), F32)]
    if mode == "ln_silu":
        in_specs += [pl.BlockSpec((1, c), row)] * 3
        args = (x, x, w) + tuple(v.reshape(1, c) for v in extra)
        scratch.append(pltpu.VMEM((tm, c), F32))
    else:
        in_specs.append(pl.BlockSpec((tm, c), lambda i: (i, 0)))
        args = (x, x, w) + tuple(extra)
    return pl.pallas_call(
        functools.partial(_conv_kernel, taps=taps, tm=tm, row_chunk=min(64, tm), lane_chunk=min(256, c),
                          tiles_per_seq=tiles_per_seq, mode=mode),
        grid=(t // tm,),
        in_specs=in_specs,
        out_specs=pl.BlockSpec((tm, c), lambda i: (i, 0)),
        out_shape=jax.ShapeDtypeStruct((t, c), BF16),
        scratch_shapes=scratch,
        compiler_params=_params(1),
        name=name,
    )(*args)


def _residual_ln(x, y, mod_ref, lng_ref, lnb_ref, alpha, gate_row):
    z = alpha * x + mod_ref[gate_row:gate_row + 1, :] * y
    return _layer_norm(z, lng_ref[...], lnb_ref[...])


def _top2_of4(a, b, c, d):
    hi1, lo1 = jnp.maximum(a, b), jnp.minimum(a, b)
    hi2, lo2 = jnp.maximum(c, d), jnp.minimum(c, d)
    return jnp.maximum(hi1, hi2) + jnp.maximum(jnp.minimum(hi1, hi2), jnp.maximum(lo1, lo2))


def _first_argmax(vals):
    idx = jnp.zeros(vals[0].shape, I32)
    best = vals[0]
    for j in range(1, len(vals)):
        take = vals[j] > best
        idx = jnp.where(take, j, idx)
        best = jnp.where(take, vals[j], best)
    return idx


def _select(rows, idx):
    out = rows[0]
    for j in range(1, len(rows)):
        out = jnp.where(idx == j, rows[j], out)
    return out


def _ln_router_kernel(x_ref, y_ref, mod_ref, lng_ref, lnb_ref, wrh_ref, wrl_ref, rb_ref, tri_ref,
                      xo_ref, h_ref, ri_ref, rw_ref, cnt_ref, carry_ref, *, alpha, n_experts):
    tm = x_ref.shape[0]
    per_group = n_experts // N_GROUPS

    @pl.when(pl.program_id(0) == 0)
    def _init():
        carry_ref[...] = jnp.zeros_like(carry_ref)

    xn = _residual_ln(x_ref[...], y_ref[...].astype(F32), mod_ref, lng_ref, lnb_ref, alpha, 2)
    xo_ref[...] = xn
    h = xn * (1.0 + mod_ref[4:5, :]) + mod_ref[3:4, :]
    h_hi = h.astype(BF16)
    h_ref[...] = h.reshape(h_ref.shape).astype(BF16)
    h_lo = (h - h_hi.astype(F32)).astype(BF16)
    logits = (jnp.dot(h_hi, wrh_ref[...], preferred_element_type=F32)
              + jnp.dot(h_lo, wrh_ref[...], preferred_element_type=F32)
              + jnp.dot(h_hi, wrl_ref[...], preferred_element_type=F32))
    lg = logits.T[0:n_experts, :]
    mx = jnp.max(lg, axis=0, keepdims=True)
    ex = jnp.exp(lg - mx)
    probs = ex / jnp.sum(ex, axis=0, keepdims=True)
    sel = probs + rb_ref[...]
    sel_rows = [sel[e:e + 1, :] for e in range(n_experts)]
    prob_rows = [probs[e:e + 1, :] for e in range(n_experts)]
    assert per_group == 4
    group_scores = [_top2_of4(*sel_rows[per_group * g:per_group * (g + 1)]) for g in range(N_GROUPS)]
    best_group = _first_argmax(group_scores)
    v = [_select([sel_rows[per_group * g + j] for g in range(N_GROUPS)], best_group) for j in range(per_group)]
    p = [_select([prob_rows[per_group * g + j] for g in range(N_GROUPS)], best_group) for j in range(per_group)]
    i1 = _first_argmax(v)
    i2 = _first_argmax([jnp.where(i1 == j, -jnp.inf, v[j]) for j in range(per_group)])
    p1, p2 = _select(p, i1), _select(p, i2)
    e1 = best_group * per_group + i1
    e2 = best_group * per_group + i2
    eid = lax.broadcasted_iota(I32, (n_experts, tm), 0)
    onehot = jnp.where((eid == e1) | (eid == e2), 1.0, 0.0)
    prefix = jnp.dot(onehot.astype(BF16), tri_ref[...], preferred_element_type=F32)
    tot = carry_ref[:, 0:1] + prefix
    rank1 = jnp.sum(jnp.where(eid == e1, tot, 0.0), axis=0, keepdims=True)
    rank2 = jnp.sum(jnp.where(eid == e2, tot, 0.0), axis=0, keepdims=True)
    new_carry = carry_ref[...] + jnp.sum(onehot, axis=1, keepdims=True)
    carry_ref[...] = new_carry
    cnt_ref[...] = new_carry.astype(I32)
    ri_ref[0:1, :] = e1
    ri_ref[1:2, :] = e2
    ri_ref[2:3, :] = rank1.astype(I32)
    ri_ref[3:4, :] = rank2.astype(I32)
    ri_ref[4:8, :] = jnp.zeros((4, tm), I32)
    rowid = lax.broadcasted_iota(I32, (LANES, tm), 0)
    wsum = p1 + p2
    slab = jnp.where(rowid == 0, p1 / wsum, jnp.where(rowid == 1, p2 / wsum, 0.0))
    rw_ref[...] = slab.T


def _ln_router(x, y, mod, layer, lng, lnb, w_router, router_bias, seq, alpha, tm):
    t, d = x.shape
    n_experts = w_router.shape[1]
    tiles_per_seq = seq // tm
    wr = jnp.zeros((d, LANES), F32).at[:, :n_experts].set(w_router)
    wr_hi = wr.astype(BF16)
    wr_lo = (wr - wr_hi.astype(F32)).astype(BF16)
    rb = jnp.broadcast_to(router_bias.astype(F32)[:, None], (n_experts, tm))
    tri = (lax.broadcasted_iota(I32, (tm, tm), 0) < lax.broadcasted_iota(I32, (tm, tm), 1)).astype(BF16)
    tok = lambda i: (i, 0)
    const = lambda i: (0, 0)
    return pl.pallas_call(
        functools.partial(_ln_router_kernel, alpha=alpha, n_experts=n_experts),
        grid=(t // tm,),
        in_specs=[
            pl.BlockSpec((tm, d), tok),
            pl.BlockSpec((tm, d), tok),
            pl.BlockSpec((None, None, N_MOD, d), lambda i: (layer, i // tiles_per_seq, 0, 0)),
            pl.BlockSpec((1, d), const),
            pl.BlockSpec((1, d), const),
            pl.BlockSpec((d, LANES), const),
            pl.BlockSpec((d, LANES), const),
            pl.BlockSpec((n_experts, tm), const),
            pl.BlockSpec((tm, tm), const),
        ],
        out_specs=[
            pl.BlockSpec((tm, d), tok),
            pl.BlockSpec((tm, d // LANES, LANES), lambda i: (i, 0, 0)),
            pl.BlockSpec((SUBLANES, tm), lambda i: (0, i)),
            pl.BlockSpec((tm, LANES), tok),
            pl.BlockSpec((n_experts, LANES), const),
        ],
        out_shape=[
            jax.ShapeDtypeStruct((t, d), F32),
            jax.ShapeDtypeStruct((t, d // LANES, LANES), BF16),
            jax.ShapeDtypeStruct((SUBLANES, t), I32),
            jax.ShapeDtypeStruct((t, LANES), F32),
            jax.ShapeDtypeStruct((n_experts, LANES), I32),
        ],
        scratch_shapes=[pltpu.VMEM((n_experts, LANES), F32)],
        compiler_params=_params(1),
        name="ln_router",
    )(x, y, mod, lng.reshape(1, d), lnb.reshape(1, d), wr_hi, wr_lo, rb, tri)


def _row_copy(src_ref, src_row, dst_ref, dst_row, sem):
    return pltpu.make_async_copy(src_ref.at[pl.ds(src_row, 1)], dst_ref.at[pl.ds(dst_row, 1)], sem)


def _dispatch_kernel(pos1_ref, pos2_ref, h_ref, xs_in_ref, xs_ref, sem, *, tm):
    del xs_in_ref
    base = pl.program_id(0) * tm

    def start(t, carry):
        _row_copy(h_ref, t, xs_ref, pos1_ref[base + t], sem).start()
        _row_copy(h_ref, t, xs_ref, pos2_ref[base + t], sem).start()
        return carry

    lax.fori_loop(0, tm, start, 0, unroll=8)
    for _ in range(2):
        pltpu.make_async_copy(h_ref, xs_ref.at[pl.ds(0, tm)], sem).wait()


def _dispatch(h, pos1, pos2, xs0, tm):
    t = h.shape[0]
    return pl.pallas_call(
        functools.partial(_dispatch_kernel, tm=tm),
        grid_spec=pltpu.PrefetchScalarGridSpec(
            num_scalar_prefetch=2,
            grid=(t // tm,),
            in_specs=[pl.BlockSpec((tm,) + h.shape[1:], lambda i, p1, p2: (i, 0, 0)),
                      pl.BlockSpec(memory_space=pl.ANY)],
            out_specs=pl.BlockSpec(memory_space=pl.ANY),
            scratch_shapes=[pltpu.SemaphoreType.DMA(())],
        ),
        out_shape=jax.ShapeDtypeStruct(xs0.shape, h.dtype),
        input_output_aliases={3: 0},
        compiler_params=_params(1),
        name="moe_dispatch",
    )(pos1, pos2, h, xs0)


def _first_tile_of_expert(te_ref, j):
    return (j == 0) | (te_ref[j] != te_ref[jnp.maximum(j - 1, 0)])


def _moe_up_kernel(te_ref, xt_ref, rows_ref, x_ref, wg_ref, wu_ref, o_ref, wb_ref):
    del xt_ref
    j = pl.program_id(1)
    half = x_ref.shape[0] // 2

    @pl.when(_first_tile_of_expert(te_ref, j))
    def _cast_weights():
        wb_ref[0] = wg_ref[...].astype(BF16)
        wb_ref[1] = wu_ref[...].astype(BF16)

    def half_tile(r0, used):
        @pl.when(used)
        def _compute():
            x = x_ref[r0:r0 + half].reshape(half, wb_ref.shape[1])
            gate = jnp.dot(x, wb_ref[0], preferred_element_type=F32)
            up = jnp.dot(x, wb_ref[1], preferred_element_type=F32)
            o_ref[r0:r0 + half, :] = (_silu(gate) * up).astype(o_ref.dtype)

        @pl.when(jnp.logical_not(used))
        def _unused():
            o_ref[r0:r0 + half, :] = jnp.zeros((half, o_ref.shape[1]), o_ref.dtype)

    half_tile(0, rows_ref[j] > 0)
    half_tile(half, rows_ref[j] > half)


def _moe_up(xs, w_gate, w_up, layer, te, xt, rows, tm, bf):
    n_rows = xs.shape[0]
    d, f = w_gate.shape[2], w_gate.shape[3]
    n_tiles = n_rows // tm
    wspec = pl.BlockSpec((None, None, d, bf), lambda jf, j, te, xt, rows: (layer, te[j], 0, jf))
    return pl.pallas_call(
        _moe_up_kernel,
        grid_spec=pltpu.PrefetchScalarGridSpec(
            num_scalar_prefetch=3,
            grid=(f // bf, n_tiles),
            in_specs=[pl.BlockSpec((tm, d // LANES, LANES), lambda jf, j, te, xt, rows: (xt[j], 0, 0)),
                      wspec, wspec],
            out_specs=pl.BlockSpec((tm, bf), lambda jf, j, te, xt, rows: (j, jf)),
            scratch_shapes=[pltpu.VMEM((2, d, bf), BF16)],
        ),
        out_shape=jax.ShapeDtypeStruct((n_rows, f), BF16),
        compiler_params=_params(2),
        name="moe_up",
    )(te, xt, rows, xs, w_gate, w_up)


def _moe_down_kernel(te_ref, xt_ref, rows_ref, x_ref, w_ref, o_ref, wb_ref):
    del xt_ref
    j = pl.program_id(1)
    half = x_ref.shape[0] // 2

    @pl.when(_first_tile_of_expert(te_ref, j))
    def _cast_weights():
        wb_ref[...] = w_ref[...].astype(BF16)

    def half_tile(r0, used):
        @pl.when(used)
        def _compute():
            acc = jnp.dot(x_ref[r0:r0 + half, :], wb_ref[...], preferred_element_type=F32)
            o_ref[r0:r0 + half] = acc.reshape((half,) + o_ref.shape[1:]).astype(o_ref.dtype)

        @pl.when(jnp.logical_not(used))
        def _unused():
            o_ref[r0:r0 + half] = jnp.zeros((half,) + o_ref.shape[1:], o_ref.dtype)

    half_tile(0, rows_ref[j] > 0)
    half_tile(half, rows_ref[j] > half)


def _moe_down(hid, w_down, layer, te, xt, rows, tm, bn):
    n_rows, f = hid.shape
    d = w_down.shape[3]
    n_tiles = n_rows // tm
    return pl.pallas_call(
        _moe_down_kernel,
        grid_spec=pltpu.PrefetchScalarGridSpec(
            num_scalar_prefetch=3,
            grid=(d // bn, n_tiles),
            in_specs=[
                pl.BlockSpec((tm, f), lambda jn, j, te, xt, rows: (xt[j], 0)),
                pl.BlockSpec((None, None, f, bn), lambda jn, j, te, xt, rows: (layer, te[j], 0, jn)),
            ],
            out_specs=pl.BlockSpec((tm, bn // LANES, LANES), lambda jn, j, te, xt, rows: (j, jn, 0)),
            scratch_shapes=[pltpu.VMEM((f, bn), BF16)],
        ),
        out_shape=jax.ShapeDtypeStruct((n_rows, d // LANES, LANES), BF16),
        compiler_params=_params(2),
        name="moe_down",
    )(te, xt, rows, hid, w_down)


def _combine_ln_kernel(pos1_ref, pos2_ref, x_ref, rw_ref, mod_ref, lng_ref, lnb_ref, nmod_ref, ys_ref,
                       *rest, alpha, emit_h):
    if emit_h:
        xo_ref, h_ref, buf_ref, sem = rest
    else:
        xo_ref, buf_ref, sem = rest
    tm = x_ref.shape[0]
    i = pl.program_id(0)

    def gather(tile, slot):
        base = tile * tm

        def start(t, carry):
            _row_copy(ys_ref, pos1_ref[base + t], buf_ref.at[slot, 0], t, sem.at[slot]).start()
            _row_copy(ys_ref, pos2_ref[base + t], buf_ref.at[slot, 1], t, sem.at[slot]).start()
            return carry

        lax.fori_loop(0, tm, start, 0, unroll=8)

    @pl.when(i == 0)
    def _first_tile():
        gather(0, 0)

    @pl.when(i + 1 < pl.num_programs(0))
    def _prefetch():
        gather(i + 1, (i + 1) % 2)

    slot = i % 2
    for k in range(2):
        pltpu.make_async_copy(ys_ref.at[pl.ds(0, tm)], buf_ref.at[slot, k], sem.at[slot]).wait()
    rw = rw_ref[...]
    y1 = buf_ref[slot, 0].reshape(x_ref.shape).astype(F32)
    y2 = buf_ref[slot, 1].reshape(x_ref.shape).astype(F32)
    y = rw[:, 0:1] * y1 + rw[:, 1:2] * y2
    xn = _residual_ln(x_ref[...], y, mod_ref, lng_ref, lnb_ref, alpha, 5)
    xo_ref[...] = xn
    if emit_h:
        h_ref[...] = (xn * (1.0 + nmod_ref[1:2, :]) + nmod_ref[0:1, :]).astype(BF16)


def _combine_ln(x, ys, pos1, pos2, rw, mod, layer, next_layer, lng, lnb, seq, alpha, tm):
    t, d = x.shape
    tiles_per_seq = seq // tm
    emit_h = next_layer is not None
    tok = lambda i, p1, p2: (i, 0)
    const = lambda i, p1, p2: (0, 0)
    mod_spec = lambda l: pl.BlockSpec((None, None, N_MOD, d), lambda i, p1, p2: (l, i // tiles_per_seq, 0, 0))
    out_specs = [pl.BlockSpec((tm, d), tok)]
    out_shape = [jax.ShapeDtypeStruct((t, d), F32)]
    if emit_h:
        out_specs.append(pl.BlockSpec((tm, d), tok))
        out_shape.append(jax.ShapeDtypeStruct((t, d), BF16))
    outs = pl.pallas_call(
        functools.partial(_combine_ln_kernel, alpha=alpha, emit_h=emit_h),
        grid_spec=pltpu.PrefetchScalarGridSpec(
            num_scalar_prefetch=2,
            grid=(t // tm,),
            in_specs=[
                pl.BlockSpec((tm, d), tok),
                pl.BlockSpec((tm, LANES), tok),
                mod_spec(layer),
                pl.BlockSpec((1, d), const),
                pl.BlockSpec((1, d), const),
                mod_spec(next_layer if emit_h else layer),
                pl.BlockSpec(memory_space=pl.ANY),
            ],
            out_specs=out_specs,
            scratch_shapes=[pltpu.VMEM((2, 2, tm, d // LANES, LANES), BF16), pltpu.SemaphoreType.DMA((2,))],
        ),
        out_shape=out_shape,
        compiler_params=_params(1),
        name="moe_combine_ln",
    )(pos1, pos2, x, rw, mod, lng.reshape(1, d), lnb.reshape(1, d), mod, ys)
    return (outs[0], outs[1]) if emit_h else (outs[0], None)


def _tiles(t, seq, d, d_ff):
    return dict(
        bm=min(1024, seq),
        bn1=min(512, d // 2),
        bn2=min(256, d // 2),
        mod_bn=min(1024, d),
        conv_tm=min(256, seq),
        ln_tm=min(256, seq),
        moe_tm=min(512, seq),
        moe_bf=min(512, d_ff),
        moe_bn=min(4096, d),
        dispatch_tm=min(512, seq),
    )


def _moe_schedule(counts, ranks, experts, tm, n_tiles):
    n_experts = counts.shape[0]
    tiles_per_expert = (counts + tm - 1) // tm
    tile_end = jnp.cumsum(tiles_per_expert)
    tile_start = tile_end - tiles_per_expert
    total = tile_end[-1]
    row_start = tile_start * tm
    pos = [row_start[experts[k]] + ranks[k] for k in range(2)]
    j = jnp.arange(n_tiles, dtype=I32)
    xt = jnp.minimum(j, total - 1)
    te = jnp.minimum(jnp.sum((xt[:, None] >= tile_end[None, :]).astype(I32), axis=1), n_experts - 1)
    rows = jnp.where(j < total, jnp.clip(counts[te] - (j - tile_start[te]) * tm, 0, tm), 0)
    return pos[0].astype(I32), pos[1].astype(I32), te.astype(I32), xt.astype(I32), rows.astype(I32)


def kernel(x, c, w_ada, b_ada, w_in, conv_a_w, conv_a_b, ln_a_g, ln_a_b, conv_b_w, w_pa, w_pb, w_o,
           ln_mix_g, ln_mix_b, w_router, router_bias, w_gate, w_up, w_down, ln_ffn_g, ln_ffn_b):
    batch, seq, d = x.shape
    n_layers = w_in.shape[0]
    d_conf = conv_a_w.shape[2]
    d_short = conv_b_w.shape[2]
    n_experts, d_ff = w_gate.shape[1], w_gate.shape[3]
    t = batch * seq
    alpha = float((2 * n_layers) ** 0.25)
    ts = _tiles(t, seq, d, d_ff)
    bm, bn1, bn2 = ts["bm"], ts["bn1"], ts["bn2"]
    c_aval, c_agate = 0, d_conf
    c_sb, c_sc, c_sx = 2 * d_conf, 2 * d_conf + d_short, 2 * d_conf + 2 * d_short
    c_ga = 2 * d_conf + 3 * d_short
    n_moe_tiles = (2 * t) // ts["moe_tm"] + n_experts
    n_moe_rows = n_moe_tiles * ts["moe_tm"]

    mod = _modulation(c, w_ada, b_ada, ts["mod_bn"])
    xc = x.reshape(t, d)
    h = _modulate(xc, mod, 0, seq, ts["ln_tm"])
    xs = jnp.zeros((n_moe_rows, d // LANES, LANES), BF16)
    for l in range(n_layers):
        w_in_l = lambda col0: (w_in, l, col0, 0)
        ya_pre = _matmul([h], [w_in_l(c_aval), w_in_l(c_agate)], [], _ep_glu, d_conf, F32, bm, bn2, "inproj_glu")
        cx = _matmul([h], [w_in_l(c_sc), w_in_l(c_sx)], [], _ep_mul, d_short, F32, bm, bn2, "inproj_cx")
        sb = _matmul([h], [w_in_l(c_sb)], [], _ep_id, d_short, BF16, bm, bn1, "inproj_sb")
        gates = _matmul([h], [w_in_l(c_ga)], [], _ep_sigmoid, 2 * d, BF16, bm, bn1, "inproj_gates")
        ya = _causal_conv(ya_pre, conv_a_w[l], seq, ts["conv_tm"], "ln_silu",
                          (conv_a_b[l], ln_a_g[l], ln_a_b[l]), "conv_a")
        yb = _causal_conv(cx, conv_b_w[l], seq, ts["conv_tm"], "gate", (sb,), "conv_b")
        m = _matmul([ya, yb], [(w_pa, l, 0, 0), (w_pb, l, 0, 1)], [(gates, 0), (gates, d)],
                    _ep_merge, d, BF16, bm, bn1, "proj_merge")
        y = _matmul([m], [(w_o, l, 0, 0)], [], _ep_id, d, BF16, bm, bn1, "out_proj")
        xc, h2, ri, rw, cnt = _ln_router(xc, y, mod, l, ln_mix_g[l], ln_mix_b[l], w_router, router_bias,
                                         seq, alpha, ts["ln_tm"])
        pos1, pos2, te, xt, rows = _moe_schedule(cnt[:, 0], (ri[2], ri[3]), (ri[0], ri[1]),
                                                 ts["moe_tm"], n_moe_tiles)
        xs = _dispatch(h2, pos1, pos2, xs, ts["dispatch_tm"])
        hid = _moe_up(xs, w_gate, w_up, l, te, xt, rows, ts["moe_tm"], ts["moe_bf"])
        ys = _moe_down(hid, w_down, l, te, xt, rows, ts["moe_tm"], ts["moe_bn"])
        xc, h = _combine_ln(xc, ys, pos1, pos2, rw, mod, l, l + 1 if l + 1 < n_layers else None,
                            ln_ffn_g[l], ln_ffn_b[l], seq, alpha, ts["ln_tm"])
    return xc.reshape(batch, seq, d)
```

```python
import functools

import jax
import jax.numpy as jnp
from jax import lax
from jax.experimental import pallas as pl
from jax.experimental.pallas import tpu as pltpu

F32 = jnp.float32
BF16 = jnp.bfloat16
I32 = jnp.int32

LN_EPS = 1e-5
N_GROUPS = 4
N_MOD = 6
V7X_VMEM_LIMIT_BYTES = 60000 * 1024
LANES = 128
SUBLANES = 8
HALO_ROWS = 32
MM_ROW_BLOCK = 256


def _params(n_grid):
    return pltpu.CompilerParams(
        dimension_semantics=("arbitrary",) * n_grid, vmem_limit_bytes=V7X_VMEM_LIMIT_BYTES)


def _sigmoid(v):
    return 1.0 / (1.0 + jnp.exp(-v))


def _silu(v):
    return v * _sigmoid(v)


def _layer_norm(z, g, b):
    mu = jnp.mean(z, axis=-1, keepdims=True)
    zc = z - mu
    var = jnp.mean(zc * zc, axis=-1, keepdims=True)
    return zc * lax.rsqrt(var + LN_EPS) * g + b


def _mod_kernel(c_ref, w_ref, b_ref, o_ref):
    cond = _silu(c_ref[...])
    acc = jnp.dot(cond.astype(BF16), w_ref[...].astype(BF16), preferred_element_type=F32)
    o_ref[...] = acc + b_ref[...]


def _modulation(c, w_ada, b_ada, bn):
    n_layers, d, n = w_ada.shape
    b = c.shape[0]
    c_pad = jnp.zeros((SUBLANES, d), F32).at[:b].set(c)
    out = pl.pallas_call(
        _mod_kernel,
        grid=(n_layers, n // bn),
        in_specs=[
            pl.BlockSpec((SUBLANES, d), lambda l, j: (0, 0)),
            pl.BlockSpec((None, d, bn), lambda l, j: (l, 0, j)),
            pl.BlockSpec((None, 1, bn), lambda l, j: (l, 0, j)),
        ],
        out_specs=pl.BlockSpec((None, SUBLANES, bn), lambda l, j: (l, 0, j)),
        out_shape=jax.ShapeDtypeStruct((n_layers, SUBLANES, n), F32),
        compiler_params=_params(2),
        name="adaln_mod",
    )(c_pad, w_ada, b_ada.reshape(n_layers, 1, n))
    return out[:, :b].reshape(n_layers, b, N_MOD, d)


def _modulate_kernel(x_ref, mod_ref, h_ref, *, shift_row, scale_row):
    sc = mod_ref[scale_row:scale_row + 1, :]
    sh = mod_ref[shift_row:shift_row + 1, :]
    h_ref[...] = (x_ref[...] * (1.0 + sc) + sh).astype(BF16)


def _modulate(x2, mod, layer, seq, tm):
    t, d = x2.shape
    tiles_per_seq = seq // tm
    return pl.pallas_call(
        functools.partial(_modulate_kernel, shift_row=0, scale_row=1),
        grid=(t // tm,),
        in_specs=[
            pl.BlockSpec((tm, d), lambda i: (i, 0)),
            pl.BlockSpec((None, None, N_MOD, d), lambda i: (layer, i // tiles_per_seq, 0, 0)),
        ],
        out_specs=pl.BlockSpec((tm, d), lambda i: (i, 0)),
        out_shape=jax.ShapeDtypeStruct((t, d), BF16),
        compiler_params=_params(1),
        name="modulate_in",
    )(x2, mod)


def _mm_kernel(*refs, n_x, w_x, n_e, epilogue):
    n_w = len(w_x)
    x_refs = refs[:n_x]
    w_refs = refs[n_x:n_x + n_w]
    e_refs = refs[n_x + n_w:n_x + n_w + n_e]
    o_ref, wb_ref = refs[-2], refs[-1]

    @pl.when(pl.program_id(1) == 0)
    def _cast_weights():
        for i in range(n_w):
            wb_ref[i] = w_refs[i][...].astype(BF16)

    rows = min(MM_ROW_BLOCK, o_ref.shape[0])
    for r0 in range(0, o_ref.shape[0], rows):
        accs = [jnp.dot(x_refs[w_x[i]][r0:r0 + rows, :], wb_ref[i], preferred_element_type=F32)
                for i in range(n_w)]
        extras = [e[r0:r0 + rows, :] for e in e_refs]
        o_ref[r0:r0 + rows, :] = epilogue(accs, extras).astype(o_ref.dtype)


def _matmul(xs, ws, extras, epilogue, n_out, out_dtype, bm, bn, name):
    m, k = xs[0].shape
    n_w = len(ws)
    in_specs = [pl.BlockSpec((bm, x.shape[1]), lambda j, i: (i, 0)) for x in xs]
    args = list(xs)
    for w, lead, col0, _ in ws:
        assert col0 % bn == 0 and w.shape[1] == k
        in_specs.append(pl.BlockSpec((None, k, bn), functools.partial(
            lambda j, i, lead, cb: (lead, 0, cb + j), lead=lead, cb=col0 // bn)))
        args.append(w)
    for e, col0 in extras:
        assert col0 % bn == 0
        in_specs.append(pl.BlockSpec((bm, bn), functools.partial(
            lambda j, i, cb: (i, cb + j), cb=col0 // bn)))
        args.append(e)
    return pl.pallas_call(
        functools.partial(_mm_kernel, n_x=len(xs), w_x=tuple(w[3] for w in ws), n_e=len(extras),
                          epilogue=epilogue),
        grid=(n_out // bn, m // bm),
        in_specs=in_specs,
        out_specs=pl.BlockSpec((bm, bn), lambda j, i: (i, j)),
        out_shape=jax.ShapeDtypeStruct((m, n_out), out_dtype),
        scratch_shapes=[pltpu.VMEM((n_w, k, bn), BF16)],
        compiler_params=_params(2),
        name=name,
    )(*args)


def _ep_glu(accs, _):
    return accs[0] * _sigmoid(accs[1])


def _ep_mul(accs, _):
    return accs[0] * accs[1]


def _ep_id(accs, _):
    return accs[0]


def _ep_sigmoid(accs, _):
    return _sigmoid(accs[0])


def _ep_merge(accs, extras):
    return extras[0].astype(F32) * accs[0] + extras[1].astype(F32) * accs[1]


def _conv_shifts(taps):
    starts = [HALO_ROWS - (taps - 1) + k for k in range(taps)]
    return starts, sorted({s % SUBLANES for s in starts} - {0})


def _conv_kernel(*refs, taps, tm, row_chunk, lane_chunk, tiles_per_seq, mode):
    if mode == "ln_silu":
        cur_ref, halo_ref, w_ref, b_ref, g_ref, beta_ref, o_ref, buf_ref, sh_ref, y_ref = refs
    else:
        cur_ref, halo_ref, w_ref, gate_ref, o_ref, buf_ref, sh_ref = refs
    c = cur_ref.shape[1]
    first = (pl.program_id(0) % tiles_per_seq) == 0
    buf_ref[0:HALO_ROWS, :] = jnp.where(first, 0.0, halo_ref[...])
    buf_ref[HALO_ROWS:HALO_ROWS + tm, :] = cur_ref[...]
    starts, residues = _conv_shifts(taps)
    sh_rows = sh_ref.shape[1]
    for n, r in enumerate(residues):
        sh_ref[n] = buf_ref[r:r + sh_rows, :]
    for r0 in range(0, tm, row_chunk):
        for c0 in range(0, c, lane_chunk):
            acc = jnp.zeros((row_chunk, lane_chunk), F32)
            for kk in range(taps):
                q, r = divmod(starts[kk], SUBLANES)
                lo = q * SUBLANES + r0
                if r == 0:
                    xk = buf_ref[lo:lo + row_chunk, c0:c0 + lane_chunk]
                else:
                    xk = sh_ref[residues.index(r), lo:lo + row_chunk, c0:c0 + lane_chunk]
                acc = acc + w_ref[kk:kk + 1, c0:c0 + lane_chunk] * xk
            if mode == "ln_silu":
                y_ref[r0:r0 + row_chunk, c0:c0 + lane_chunk] = acc + b_ref[:, c0:c0 + lane_chunk]
            else:
                gate = gate_ref[r0:r0 + row_chunk, c0:c0 + lane_chunk].astype(F32)
                o_ref[r0:r0 + row_chunk, c0:c0 + lane_chunk] = (gate * acc).astype(o_ref.dtype)
    if mode == "ln_silu":
        o_ref[...] = _silu(_layer_norm(y_ref[...], g_ref[...], beta_ref[...])).astype(o_ref.dtype)


def _causal_conv(x, w, seq, tm, mode, extra, name):
    t, c = x.shape
    taps = w.shape[0]
    assert taps - 1 <= HALO_ROWS and tm % HALO_ROWS == 0
    tiles_per_seq = seq // tm
    halo_blocks = tm // HALO_ROWS
    row = lambda i: (0, 0)
    in_specs = [
        pl.BlockSpec((tm, c), lambda i: (i, 0)),
        pl.BlockSpec((HALO_ROWS, c), lambda i: (jnp.maximum(i * halo_blocks - 1, 0), 0)),
        pl.BlockSpec((taps, c), row),
    ]
    _, residues = _conv_shifts(taps)
    scratch = [pltpu.VMEM((HALO_ROWS + tm, c), F32),
               pltpu.VMEM((max(len(residues), 1), HALO_ROWS + tm - SUBLANES, c), F32)]
    if mode == "ln_silu":
        in_specs += [pl.BlockSpec((1, c), row)] * 3
        args = (x, x, w) + tuple(v.reshape(1, c) for v in extra)
        scratch.append(pltpu.VMEM((tm, c), F32))
    else:
        in_specs.append(pl.BlockSpec((tm, c), lambda i: (i, 0)))
        args = (x, x, w) + tuple(extra)
    return pl.pallas_call(
        functools.partial(_conv_kernel, taps=taps, tm=tm, row_chunk=min(64, tm), lane_chunk=min(256, c),
                          tiles_per_seq=tiles_per_seq, mode=mode),
        grid=(t // tm,),
        in_specs=in_specs,
        out_specs=pl.BlockSpec((tm, c), lambda i: (i, 0)),
        out_shape=jax.ShapeDtypeStruct((t, c), BF16),
        scratch_shapes=scratch,
        compiler_params=_params(1),
        name=name,
    )(*args)


def _residual_ln(x, y, mod_ref, lng_ref, lnb_ref, alpha, gate_row):
    z = alpha * x + mod_ref[gate_row:gate_row + 1, :] * y
    return _layer_norm(z, lng_ref[...], lnb_ref[...])


def _top2_of4(a, b, c, d):
    hi1, lo1 = jnp.maximum(a, b), jnp.minimum(a, b)
    hi2, lo2 = jnp.maximum(c, d), jnp.minimum(c, d)
    return jnp.maximum(hi1, hi2) + jnp.maximum(jnp.minimum(hi1, hi2), jnp.maximum(lo1, lo2))


def _first_argmax(vals):
    idx = jnp.zeros(vals[0].shape, I32)
    best = vals[0]
    for j in range(1, len(vals)):
        take = vals[j] > best
        idx = jnp.where(take, j, idx)
        best = jnp.where(take, vals[j], best)
    return idx


def _select(rows, idx):
    out = rows[0]
    for j in range(1, len(rows)):
        out = jnp.where(idx == j, rows[j], out)
    return out


def _ln_router_kernel(x_ref, y_ref, mod_ref, lng_ref, lnb_ref, wrh_ref, wrl_ref, rb_ref, tri_ref,
                      xo_ref, h_ref, ri_ref, rw_ref, cnt_ref, carry_ref, *, alpha, n_experts):
    tm = x_ref.shape[0]
    per_group = n_experts // N_GROUPS

    @pl.when(pl.program_id(0) == 0)
    def _init():
        carry_ref[...] = jnp.zeros_like(carry_ref)

    xn = _residual_ln(x_ref[...], y_ref[...].astype(F32), mod_ref, lng_ref, lnb_ref, alpha, 2)
    xo_ref[...] = xn
    h = xn * (1.0 + mod_ref[4:5, :]) + mod_ref[3:4, :]
    h_hi = h.astype(BF16)
    h_ref[...] = h.reshape(h_ref.shape).astype(BF16)
    h_lo = (h - h_hi.astype(F32)).astype(BF16)
    logits = (jnp.dot(h_hi, wrh_ref[...], preferred_element_type=F32)
              + jnp.dot(h_lo, wrh_ref[...], preferred_element_type=F32)
              + jnp.dot(h_hi, wrl_ref[...], preferred_element_type=F32))
    lg = logits.T[0:n_experts, :]
    mx = jnp.max(lg, axis=0, keepdims=True)
    ex = jnp.exp(lg - mx)
    probs = ex / jnp.sum(ex, axis=0, keepdims=True)
    sel = probs + rb_ref[...]
    sel_rows = [sel[e:e + 1, :] for e in range(n_experts)]
    prob_rows = [probs[e:e + 1, :] for e in range(n_experts)]
    assert per_group == 4
    group_scores = [_top2_of4(*sel_rows[per_group * g:per_group * (g + 1)]) for g in range(N_GROUPS)]
    best_group = _first_argmax(group_scores)
    v = [_select([sel_rows[per_group * g + j] for g in range(N_GROUPS)], best_group) for j in range(per_group)]
    p = [_select([prob_rows[per_group * g + j] for g in range(N_GROUPS)], best_group) for j in range(per_group)]
    i1 = _first_argmax(v)
    i2 = _first_argmax([jnp.where(i1 == j, -jnp.inf, v[j]) for j in range(per_group)])
    p1, p2 = _select(p, i1), _select(p, i2)
    e1 = best_group * per_group + i1
    e2 = best_group * per_group + i2
    eid = lax.broadcasted_iota(I32, (n_experts, tm), 0)
    onehot = jnp.where((eid == e1) | (eid == e2), 1.0, 0.0)
    prefix = jnp.dot(onehot.astype(BF16), tri_ref[...], preferred_element_type=F32)
    tot = carry_ref[:, 0:1] + prefix
    rank1 = jnp.sum(jnp.where(eid == e1, tot, 0.0), axis=0, keepdims=True)
    rank2 = jnp.sum(jnp.where(eid == e2, tot, 0.0), axis=0, keepdims=True)
    new_carry = carry_ref[...] + jnp.sum(onehot, axis=1, keepdims=True)
    carry_ref[...] = new_carry
    cnt_ref[...] = new_carry.astype(I32)
    ri_ref[0:1, :] = e1
    ri_ref[1:2, :] = e2
    ri_ref[2:3, :] = rank1.astype(I32)
    ri_ref[3:4, :] = rank2.astype(I32)
    ri_ref[4:8, :] = jnp.zeros((4, tm), I32)
    rowid = lax.broadcasted_iota(I32, (LANES, tm), 0)
    wsum = p1 + p2
    slab = jnp.where(rowid == 0, p1 / wsum, jnp.where(rowid == 1, p2 / wsum, 0.0))
    rw_ref[...] = slab.T


def _ln_router(x, y, mod, layer, lng, lnb, w_router, router_bias, seq, alpha, tm):
    t, d = x.shape
    n_experts = w_router.shape[1]
    tiles_per_seq = seq // tm
    wr = jnp.zeros((d, LANES), F32).at[:, :n_experts].set(w_router)
    wr_hi = wr.astype(BF16)
    wr_lo = (wr - wr_hi.astype(F32)).astype(BF16)
    rb = jnp.broadcast_to(router_bias.astype(F32)[:, None], (n_experts, tm))
    tri = (lax.broadcasted_iota(I32, (tm, tm), 0) < lax.broadcasted_iota(I32, (tm, tm), 1)).astype(BF16)
    tok = lambda i: (i, 0)
    const = lambda i: (0, 0)
    return pl.pallas_call(
        functools.partial(_ln_router_kernel, alpha=alpha, n_experts=n_experts),
        grid=(t // tm,),
        in_specs=[
            pl.BlockSpec((tm, d), tok),
            pl.BlockSpec((tm, d), tok),
            pl.BlockSpec((None, None, N_MOD, d), lambda i: (layer, i // tiles_per_seq, 0, 0)),
            pl.BlockSpec((1, d), const),
            pl.BlockSpec((1, d), const),
            pl.BlockSpec((d, LANES), const),
            pl.BlockSpec((d, LANES), const),
            pl.BlockSpec((n_experts, tm), const),
            pl.BlockSpec((tm, tm), const),
        ],
        out_specs=[
            pl.BlockSpec((tm, d), tok),
            pl.BlockSpec((tm, d // LANES, LANES), lambda i: (i, 0, 0)),
            pl.BlockSpec((SUBLANES, tm), lambda i: (0, i)),
            pl.BlockSpec((tm, LANES), tok),
            pl.BlockSpec((n_experts, LANES), const),
        ],
        out_shape=[
            jax.ShapeDtypeStruct((t, d), F32),
            jax.ShapeDtypeStruct((t, d // LANES, LANES), BF16),
            jax.ShapeDtypeStruct((SUBLANES, t), I32),
            jax.ShapeDtypeStruct((t, LANES), F32),
            jax.ShapeDtypeStruct((n_experts, LANES), I32),
        ],
        scratch_shapes=[pltpu.VMEM((n_experts, LANES), F32)],
        compiler_params=_params(1),
        name="ln_router",
    )(x, y, mod, lng.reshape(1, d), lnb.reshape(1, d), wr_hi, wr_lo, rb, tri)


def _row_copy(src_ref, src_row, dst_ref, dst_row, sem):
    return pltpu.make_async_copy(src_ref.at[pl.ds(src_row, 1)], dst_ref.at[pl.ds(dst_row, 1)], sem)


def _dispatch_kernel(pos1_ref, pos2_ref, h_ref, xs_in_ref, xs_ref, sem, *, tm):
    del xs_in_ref
    base = pl.program_id(0) * tm

    def start(t, carry):
        _row_copy(h_ref, t, xs_ref, pos1_ref[base + t], sem).start()
        _row_copy(h_ref, t, xs_ref, pos2_ref[base + t], sem).start()
        return carry

    lax.fori_loop(0, tm, start, 0, unroll=8)
    for _ in range(2):
        pltpu.make_async_copy(h_ref, xs_ref.at[pl.ds(0, tm)], sem).wait()


def _dispatch(h, pos1, pos2, xs0, tm):
    t = h.shape[0]
    return pl.pallas_call(
        functools.partial(_dispatch_kernel, tm=tm),
        grid_spec=pltpu.PrefetchScalarGridSpec(
            num_scalar_prefetch=2,
            grid=(t // tm,),
            in_specs=[pl.BlockSpec((tm,) + h.shape[1:], lambda i, p1, p2: (i, 0, 0)),
                      pl.BlockSpec(memory_space=pl.ANY)],
            out_specs=pl.BlockSpec(memory_space=pl.ANY),
            scratch_shapes=[pltpu.SemaphoreType.DMA(())],
        ),
        out_shape=jax.ShapeDtypeStruct(xs0.shape, h.dtype),
        input_output_aliases={3: 0},
        compiler_params=_params(1),
        name="moe_dispatch",
    )(pos1, pos2, h, xs0)


def _first_tile_of_expert(te_ref, j):
    return (j == 0) | (te_ref[j] != te_ref[jnp.maximum(j - 1, 0)])


def _expert_weight_copies(w_refs, layer, expert, col0, stage_ref, sems):
    width = stage_ref.shape[2]
    return [pltpu.make_async_copy(w.at[layer, expert].at[:, pl.ds(col0, width)], stage_ref.at[i], sems.at[i])
            for i, w in enumerate(w_refs)]


def _stream_expert_weights(te_ref, nxt_ref, w_refs, layer, stage_ref, wb_ref, sems):
    p, j = pl.program_id(0), pl.program_id(1)
    width = stage_ref.shape[2]

    def copies(expert, col_pass):
        return _expert_weight_copies(w_refs, layer, expert, pl.multiple_of(col_pass * width, width),
                                     stage_ref, sems)

    @pl.when(_first_tile_of_expert(te_ref, j))
    def _switch_expert():
        @pl.when((p == 0) & (j == 0))
        def _very_first():
            for c in copies(te_ref[0], 0):
                c.start()

        for c in copies(te_ref[j], p):
            c.wait()
        for i in range(len(w_refs)):
            wb_ref[i] = stage_ref[i].astype(BF16)
        nxt = nxt_ref[j]

        @pl.when(nxt >= 0)
        def _next_expert():
            for c in copies(nxt, p):
                c.start()

        @pl.when((nxt < 0) & (p + 1 < pl.num_programs(0)))
        def _next_pass():
            for c in copies(te_ref[0], p + 1):
                c.start()


def _moe_up_kernel(te_ref, xt_ref, rows_ref, nxt_ref, x_ref, wg_ref, wu_ref, o_ref, stage_ref, wb_ref, sems,
                   *, layer):
    del xt_ref
    j = pl.program_id(1)
    half = x_ref.shape[0] // 2
    _stream_expert_weights(te_ref, nxt_ref, (wg_ref, wu_ref), layer, stage_ref, wb_ref, sems)

    def half_tile(r0, used):
        @pl.when(used)
        def _compute():
            x = x_ref[r0:r0 + half].reshape(half, wb_ref.shape[1])
            gate = jnp.dot(x, wb_ref[0], preferred_element_type=F32)
            up = jnp.dot(x, wb_ref[1], preferred_element_type=F32)
            o_ref[r0:r0 + half, :] = (_silu(gate) * up).astype(o_ref.dtype)

        @pl.when(jnp.logical_not(used))
        def _unused():
            o_ref[r0:r0 + half, :] = jnp.zeros((half, o_ref.shape[1]), o_ref.dtype)

    half_tile(0, rows_ref[j] > 0)
    half_tile(half, rows_ref[j] > half)


def _moe_up(xs, w_gate, w_up, layer, te, xt, rows, nxt, tm, bf):
    n_rows = xs.shape[0]
    d, f = w_gate.shape[2], w_gate.shape[3]
    n_tiles = n_rows // tm
    return pl.pallas_call(
        functools.partial(_moe_up_kernel, layer=layer),
        grid_spec=pltpu.PrefetchScalarGridSpec(
            num_scalar_prefetch=4,
            grid=(f // bf, n_tiles),
            in_specs=[pl.BlockSpec((tm, d // LANES, LANES), lambda jf, j, te, xt, rows, nxt: (xt[j], 0, 0)),
                      pl.BlockSpec(memory_space=pl.ANY), pl.BlockSpec(memory_space=pl.ANY)],
            out_specs=pl.BlockSpec((tm, bf), lambda jf, j, te, xt, rows, nxt: (j, jf)),
            scratch_shapes=[pltpu.VMEM((2, d, bf), F32), pltpu.VMEM((2, d, bf), BF16),
                            pltpu.SemaphoreType.DMA((2,))],
        ),
        out_shape=jax.ShapeDtypeStruct((n_rows, f), BF16),
        compiler_params=_params(2),
        name="moe_up",
    )(te, xt, rows, nxt, xs, w_gate, w_up)


def _moe_down_kernel(te_ref, xt_ref, rows_ref, nxt_ref, x_ref, w_ref, o_ref, stage_ref, wb_ref, sems, *, layer):
    del xt_ref
    j = pl.program_id(1)
    half = x_ref.shape[0] // 2
    _stream_expert_weights(te_ref, nxt_ref, (w_ref,), layer, stage_ref, wb_ref, sems)

    def half_tile(r0, used):
        @pl.when(used)
        def _compute():
            acc = jnp.dot(x_ref[r0:r0 + half, :], wb_ref[0], preferred_element_type=F32)
            o_ref[r0:r0 + half] = acc.reshape((half,) + o_ref.shape[1:]).astype(o_ref.dtype)

        @pl.when(jnp.logical_not(used))
        def _unused():
            o_ref[r0:r0 + half] = jnp.zeros((half,) + o_ref.shape[1:], o_ref.dtype)

    half_tile(0, rows_ref[j] > 0)
    half_tile(half, rows_ref[j] > half)


def _moe_down(hid, w_down, layer, te, xt, rows, nxt, tm, bn):
    n_rows, f = hid.shape
    d = w_down.shape[3]
    n_tiles = n_rows // tm
    return pl.pallas_call(
        functools.partial(_moe_down_kernel, layer=layer),
        grid_spec=pltpu.PrefetchScalarGridSpec(
            num_scalar_prefetch=4,
            grid=(d // bn, n_tiles),
            in_specs=[
                pl.BlockSpec((tm, f), lambda jn, j, te, xt, rows, nxt: (xt[j], 0)),
                pl.BlockSpec(memory_space=pl.ANY),
            ],
            out_specs=pl.BlockSpec((tm, bn // LANES, LANES), lambda jn, j, te, xt, rows, nxt: (j, jn, 0)),
            scratch_shapes=[pltpu.VMEM((1, f, bn), F32), pltpu.VMEM((1, f, bn), BF16),
                            pltpu.SemaphoreType.DMA((1,))],
        ),
        out_shape=jax.ShapeDtypeStruct((n_rows, d // LANES, LANES), BF16),
        compiler_params=_params(2),
        name="moe_down",
    )(te, xt, rows, nxt, hid, w_down)


def _combine_ln_kernel(pos1_ref, pos2_ref, x_ref, rw_ref, mod_ref, lng_ref, lnb_ref, nmod_ref, ys_ref,
                       *rest, alpha, emit_h):
    if emit_h:
        xo_ref, h_ref, buf_ref, sem = rest
    else:
        xo_ref, buf_ref, sem = rest
    tm = x_ref.shape[0]
    i = pl.program_id(0)

    def gather(tile, slot):
        base = tile * tm

        def start(t, carry):
            _row_copy(ys_ref, pos1_ref[base + t], buf_ref.at[slot, 0], t, sem.at[slot]).start()
            _row_copy(ys_ref, pos2_ref[base + t], buf_ref.at[slot, 1], t, sem.at[slot]).start()
            return carry

        lax.fori_loop(0, tm, start, 0, unroll=8)

    @pl.when(i == 0)
    def _first_tile():
        gather(0, 0)

    @pl.when(i + 1 < pl.num_programs(0))
    def _prefetch():
        gather(i + 1, (i + 1) % 2)

    slot = i % 2
    for k in range(2):
        pltpu.make_async_copy(ys_ref.at[pl.ds(0, tm)], buf_ref.at[slot, k], sem.at[slot]).wait()
    rw = rw_ref[...]
    y1 = buf_ref[slot, 0].reshape(x_ref.shape).astype(F32)
    y2 = buf_ref[slot, 1].reshape(x_ref.shape).astype(F32)
    y = rw[:, 0:1] * y1 + rw[:, 1:2] * y2
    xn = _residual_ln(x_ref[...], y, mod_ref, lng_ref, lnb_ref, alpha, 5)
    xo_ref[...] = xn
    if emit_h:
        h_ref[...] = (xn * (1.0 + nmod_ref[1:2, :]) + nmod_ref[0:1, :]).astype(BF16)


def _combine_ln(x, ys, pos1, pos2, rw, mod, layer, next_layer, lng, lnb, seq, alpha, tm):
    t, d = x.shape
    tiles_per_seq = seq // tm
    emit_h = next_layer is not None
    tok = lambda i, p1, p2: (i, 0)
    const = lambda i, p1, p2: (0, 0)
    mod_spec = lambda l: pl.BlockSpec((None, None, N_MOD, d), lambda i, p1, p2: (l, i // tiles_per_seq, 0, 0))
    out_specs = [pl.BlockSpec((tm, d), tok)]
    out_shape = [jax.ShapeDtypeStruct((t, d), F32)]
    if emit_h:
        out_specs.append(pl.BlockSpec((tm, d), tok))
        out_shape.append(jax.ShapeDtypeStruct((t, d), BF16))
    outs = pl.pallas_call(
        functools.partial(_combine_ln_kernel, alpha=alpha, emit_h=emit_h),
        grid_spec=pltpu.PrefetchScalarGridSpec(
            num_scalar_prefetch=2,
            grid=(t // tm,),
            in_specs=[
                pl.BlockSpec((tm, d), tok),
                pl.BlockSpec((tm, LANES), tok),
                mod_spec(layer),
                pl.BlockSpec((1, d), const),
                pl.BlockSpec((1, d), const),
                mod_spec(next_layer if emit_h else layer),
                pl.BlockSpec(memory_space=pl.ANY),
            ],
            out_specs=out_specs,
            scratch_shapes=[pltpu.VMEM((2, 2, tm, d // LANES, LANES), BF16), pltpu.SemaphoreType.DMA((2,))],
        ),
        out_shape=out_shape,
        compiler_params=_params(1),
        name="moe_combine_ln",
    )(pos1, pos2, x, rw, mod, lng.reshape(1, d), lnb.reshape(1, d), mod, ys)
    return (outs[0], outs[1]) if emit_h else (outs[0], None)


def _tiles(t, seq, d, d_ff):
    return dict(
        bm=min(1024, seq),
        bn1=min(512, d // 2),
        bn2=min(256, d // 2),
        mod_bn=min(1024, d),
        conv_tm=min(256, seq),
        ln_tm=min(256, seq),
        moe_tm=min(512, seq),
        moe_bf=min(512, d_ff),
        moe_bn=min(4096, d),
        dispatch_tm=min(512, seq),
    )


def _moe_schedule(counts, ranks, experts, tm, n_tiles):
    n_experts = counts.shape[0]
    tiles_per_expert = (counts + tm - 1) // tm
    tile_end = jnp.cumsum(tiles_per_expert)
    tile_start = tile_end - tiles_per_expert
    total = tile_end[-1]
    row_start = tile_start * tm
    pos = [row_start[experts[k]] + ranks[k] for k in range(2)]
    j = jnp.arange(n_tiles, dtype=I32)
    xt = jnp.minimum(j, total - 1)
    te = jnp.minimum(jnp.sum((xt[:, None] >= tile_end[None, :]).astype(I32), axis=1), n_experts - 1)
    rows = jnp.where(j < total, jnp.clip(counts[te] - (j - tile_start[te]) * tm, 0, tm), 0)
    eid = jnp.arange(n_experts, dtype=I32)
    later = jnp.where((eid[None, :] > eid[:, None]) & (tiles_per_expert[None, :] > 0), eid[None, :], n_experts)
    next_expert = jnp.min(later, axis=1)
    nxt = jnp.where(next_expert < n_experts, next_expert, -1)[te]
    return (pos[0].astype(I32), pos[1].astype(I32), te.astype(I32), xt.astype(I32), rows.astype(I32),
            nxt.astype(I32))


def kernel(x, c, w_ada, b_ada, w_in, conv_a_w, conv_a_b, ln_a_g, ln_a_b, conv_b_w, w_pa, w_pb, w_o,
           ln_mix_g, ln_mix_b, w_router, router_bias, w_gate, w_up, w_down, ln_ffn_g, ln_ffn_b):
    batch, seq, d = x.shape
    n_layers = w_in.shape[0]
    d_conf = conv_a_w.shape[2]
    d_short = conv_b_w.shape[2]
    n_experts, d_ff = w_gate.shape[1], w_gate.shape[3]
    t = batch * seq
    alpha = float((2 * n_layers) ** 0.25)
    ts = _tiles(t, seq, d, d_ff)
    bm, bn1, bn2 = ts["bm"], ts["bn1"], ts["bn2"]
    c_aval, c_agate = 0, d_conf
    c_sb, c_sc, c_sx = 2 * d_conf, 2 * d_conf + d_short, 2 * d_conf + 2 * d_short
    c_ga = 2 * d_conf + 3 * d_short
    n_moe_tiles = (2 * t) // ts["moe_tm"] + n_experts
    n_moe_rows = n_moe_tiles * ts["moe_tm"]

    mod = _modulation(c, w_ada, b_ada, ts["mod_bn"])
    xc = x.reshape(t, d)
    h = _modulate(xc, mod, 0, seq, ts["ln_tm"])
    xs = jnp.zeros((n_moe_rows, d // LANES, LANES), BF16)
    for l in range(n_layers):
        w_in_l = lambda col0: (w_in, l, col0, 0)
        ya_pre = _matmul([h], [w_in_l(c_aval), w_in_l(c_agate)], [], _ep_glu, d_conf, F32, bm, bn2, "inproj_glu")
        cx = _matmul([h], [w_in_l(c_sc), w_in_l(c_sx)], [], _ep_mul, d_short, F32, bm, bn2, "inproj_cx")
        sb = _matmul([h], [w_in_l(c_sb)], [], _ep_id, d_short, BF16, bm, bn1, "inproj_sb")
        gates = _matmul([h], [w_in_l(c_ga)], [], _ep_sigmoid, 2 * d, BF16, bm, bn1, "inproj_gates")
        ya = _causal_conv(ya_pre, conv_a_w[l], seq, ts["conv_tm"], "ln_silu",
                          (conv_a_b[l], ln_a_g[l], ln_a_b[l]), "conv_a")
        yb = _causal_conv(cx, conv_b_w[l], seq, ts["conv_tm"], "gate", (sb,), "conv_b")
        m = _matmul([ya, yb], [(w_pa, l, 0, 0), (w_pb, l, 0, 1)], [(gates, 0), (gates, d)],
                    _ep_merge, d, BF16, bm, bn1, "proj_merge")
        y = _matmul([m], [(w_o, l, 0, 0)], [], _ep_id, d, BF16, bm, bn1, "out_proj")
        xc, h2, ri, rw, cnt = _ln_router(xc, y, mod, l, ln_mix_g[l], ln_mix_b[l], w_router, router_bias,
                                         seq, alpha, ts["ln_tm"])
        pos1, pos2, te, xt, rows, nxt = _moe_schedule(cnt[:, 0], (ri[2], ri[3]), (ri[0], ri[1]),
                                                      ts["moe_tm"], n_moe_tiles)
        xs = _dispatch(h2, pos1, pos2, xs, ts["dispatch_tm"])
        hid = _moe_up(xs, w_gate, w_up, l, te, xt, rows, nxt, ts["moe_tm"], ts["moe_bf"])
        ys = _moe_down(hid, w_down, l, te, xt, rows, nxt, ts["moe_tm"], ts["moe_bn"])
        xc, h = _combine_ln(xc, ys, pos1, pos2, rw, mod, l, l + 1 if l + 1 < n_layers else None,
                            ln_ffn_g[l], ln_ffn_b[l], seq, alpha, ts["ln_tm"])
    return xc.reshape(batch, seq, d)
```

```python
import functools

import jax
import jax.numpy as jnp
from jax import lax
from jax.experimental import pallas as pl
from jax.experimental.pallas import tpu as pltpu

F32 = jnp.float32
BF16 = jnp.bfloat16
I32 = jnp.int32

LN_EPS = 1e-5
N_GROUPS = 4
N_MOD = 6
V7X_VMEM_LIMIT_BYTES = 60000 * 1024
LANES = 128
SUBLANES = 8
HALO_ROWS = 32
MM_ROW_BLOCK = 256
CONV_ROW_CHUNK = 128


def _params(n_grid):
    return pltpu.CompilerParams(
        dimension_semantics=("arbitrary",) * n_grid, vmem_limit_bytes=V7X_VMEM_LIMIT_BYTES)


def _sigmoid(v):
    return 1.0 / (1.0 + jnp.exp(-v))


def _silu(v):
    return v * _sigmoid(v)


def _layer_norm(z, g, b):
    mu = jnp.mean(z, axis=-1, keepdims=True)
    zc = z - mu
    var = jnp.mean(zc * zc, axis=-1, keepdims=True)
    return zc * lax.rsqrt(var + LN_EPS) * g + b


def _mod_kernel(c_ref, w_ref, b_ref, o_ref):
    cond = _silu(c_ref[...])
    acc = jnp.dot(cond.astype(BF16), w_ref[...].astype(BF16), preferred_element_type=F32)
    o_ref[...] = acc + b_ref[...]


def _modulation(c, w_ada, b_ada, bn):
    n_layers, d, n = w_ada.shape
    b = c.shape[0]
    c_pad = jnp.zeros((SUBLANES, d), F32).at[:b].set(c)
    out = pl.pallas_call(
        _mod_kernel,
        grid=(n_layers, n // bn),
        in_specs=[
            pl.BlockSpec((SUBLANES, d), lambda l, j: (0, 0)),
            pl.BlockSpec((None, d, bn), lambda l, j: (l, 0, j)),
            pl.BlockSpec((None, 1, bn), lambda l, j: (l, 0, j)),
        ],
        out_specs=pl.BlockSpec((None, SUBLANES, bn), lambda l, j: (l, 0, j)),
        out_shape=jax.ShapeDtypeStruct((n_layers, SUBLANES, n), F32),
        compiler_params=_params(2),
        name="adaln_mod",
    )(c_pad, w_ada, b_ada.reshape(n_layers, 1, n))
    return out[:, :b].reshape(n_layers, b, N_MOD, d)


def _modulate_kernel(x_ref, mod_ref, h_ref, *, shift_row, scale_row):
    sc = mod_ref[scale_row:scale_row + 1, :]
    sh = mod_ref[shift_row:shift_row + 1, :]
    h_ref[...] = (x_ref[...] * (1.0 + sc) + sh).astype(BF16)


def _modulate(x2, mod, layer, seq, tm):
    t, d = x2.shape
    tiles_per_seq = seq // tm
    return pl.pallas_call(
        functools.partial(_modulate_kernel, shift_row=0, scale_row=1),
        grid=(t // tm,),
        in_specs=[
            pl.BlockSpec((tm, d), lambda i: (i, 0)),
            pl.BlockSpec((None, None, N_MOD, d), lambda i: (layer, i // tiles_per_seq, 0, 0)),
        ],
        out_specs=pl.BlockSpec((tm, d), lambda i: (i, 0)),
        out_shape=jax.ShapeDtypeStruct((t, d), BF16),
        compiler_params=_params(1),
        name="modulate_in",
    )(x2, mod)


def _mm_kernel(*refs, n_x, w_cols, w_x, n_e, epilogue):
    n_w = len(w_x)
    x_refs = refs[:n_x]
    w_refs = refs[n_x:n_x + n_w]
    e_refs = refs[n_x + n_w:n_x + n_w + n_e]
    o_ref, stage_ref, wb_ref, sems = refs[n_x + n_w + n_e:]
    j, i = pl.program_id(0), pl.program_id(1)
    bn = wb_ref.shape[2]

    def copies(col_tile):
        return [pltpu.make_async_copy(
            w.at[lead].at[:, pl.ds(pl.multiple_of((cb + col_tile) * bn, bn), bn)], stage_ref.at[k], sems.at[k])
            for k, (w, (lead, cb)) in enumerate(zip(w_refs, w_cols))]

    @pl.when(i == 0)
    def _next_column_tile():
        @pl.when(j == 0)
        def _very_first():
            for c in copies(0):
                c.start()

        for c in copies(j):
            c.wait()
        for k in range(n_w):
            wb_ref[k] = stage_ref[k].astype(BF16)

        @pl.when(j + 1 < pl.num_programs(0))
        def _prefetch():
            for c in copies(j + 1):
                c.start()

    rows = min(MM_ROW_BLOCK, o_ref.shape[0])
    for r0 in range(0, o_ref.shape[0], rows):
        accs = [jnp.dot(x_refs[w_x[k]][r0:r0 + rows, :], wb_ref[k], preferred_element_type=F32)
                for k in range(n_w)]
        extras = [e[r0:r0 + rows, :] for e in e_refs]
        o_ref[r0:r0 + rows, :] = epilogue(accs, extras).astype(o_ref.dtype)


def _matmul(xs, ws, extras, epilogue, n_out, out_dtype, bm, bn, name):
    m, k = xs[0].shape
    n_w = len(ws)
    in_specs = [pl.BlockSpec((bm, x.shape[1]), lambda j, i: (i, 0)) for x in xs]
    args = list(xs)
    for w, lead, col0, _ in ws:
        assert col0 % bn == 0 and w.shape[1] == k
        in_specs.append(pl.BlockSpec(memory_space=pl.ANY))
        args.append(w)
    for e, col0 in extras:
        assert col0 % bn == 0
        in_specs.append(pl.BlockSpec((bm, bn), functools.partial(
            lambda j, i, cb: (i, cb + j), cb=col0 // bn)))
        args.append(e)
    return pl.pallas_call(
        functools.partial(_mm_kernel, n_x=len(xs), w_cols=tuple((w[1], w[2] // bn) for w in ws),
                          w_x=tuple(w[3] for w in ws), n_e=len(extras), epilogue=epilogue),
        grid=(n_out // bn, m // bm),
        in_specs=in_specs,
        out_specs=pl.BlockSpec((bm, bn), lambda j, i: (i, j)),
        out_shape=jax.ShapeDtypeStruct((m, n_out), out_dtype),
        scratch_shapes=[pltpu.VMEM((n_w, k, bn), F32), pltpu.VMEM((n_w, k, bn), BF16),
                        pltpu.SemaphoreType.DMA((n_w,))],
        compiler_params=_params(2),
        name=name,
    )(*args)


def _ep_glu(accs, _):
    return accs[0] * _sigmoid(accs[1])


def _ep_mul(accs, _):
    return accs[0] * accs[1]


def _ep_id(accs, _):
    return accs[0]


def _ep_sigmoid(accs, _):
    return _sigmoid(accs[0])


def _ep_merge(accs, extras):
    return extras[0].astype(F32) * accs[0] + extras[1].astype(F32) * accs[1]


def _conv_shifts(taps):
    starts = [HALO_ROWS - (taps - 1) + k for k in range(taps)]
    return starts, sorted({s % SUBLANES for s in starts} - {0})


def _conv_kernel(*refs, taps, tm, tiles_per_seq, mode):
    if mode == "ln_silu":
        cur_ref, halo_ref, w_ref, b_ref, g_ref, beta_ref, o_ref, buf_ref, sh_ref, y_ref = refs
    else:
        cur_ref, halo_ref, w_ref, gate_ref, o_ref, buf_ref, sh_ref = refs
    c = cur_ref.shape[1]
    first = (pl.program_id(0) % tiles_per_seq) == 0
    buf_ref[0:HALO_ROWS, :] = jnp.where(first, 0.0, halo_ref[...])
    buf_ref[HALO_ROWS:HALO_ROWS + tm, :] = cur_ref[...]
    starts, residues = _conv_shifts(taps)
    sh_rows = sh_ref.shape[1]
    for n, r in enumerate(residues):
        sh_ref[n] = buf_ref[r:r + sh_rows, :]
    row_chunk = min(CONV_ROW_CHUNK, tm)
    n_vregs = row_chunk // SUBLANES
    n_lane_chunks = c // LANES

    def chunk(idx, carry):
        r0 = pl.multiple_of((idx // n_lane_chunks) * row_chunk, row_chunk)
        lanes = pl.ds(pl.multiple_of((idx % n_lane_chunks) * LANES, LANES), LANES)
        accs = [jnp.zeros((SUBLANES, LANES), F32)] * n_vregs
        for kk in range(taps):
            q, r = divmod(starts[kk], SUBLANES)
            wk = jnp.broadcast_to(w_ref[kk:kk + 1, lanes], (SUBLANES, LANES))
            src = buf_ref if r == 0 else sh_ref.at[residues.index(r)]
            for v in range(n_vregs):
                rows = pl.ds(pl.multiple_of(r0 + (q + v) * SUBLANES, SUBLANES), SUBLANES)
                accs[v] = accs[v] + wk * src[rows, lanes]
        if mode == "ln_silu":
            bias = jnp.broadcast_to(b_ref[:, lanes], (SUBLANES, LANES))
            for v in range(n_vregs):
                y_ref[pl.ds(pl.multiple_of(r0 + v * SUBLANES, SUBLANES), SUBLANES), lanes] = accs[v] + bias
        else:
            for v in range(0, n_vregs, 2):
                rows = pl.ds(pl.multiple_of(r0 + v * SUBLANES, 2 * SUBLANES), 2 * SUBLANES)
                acc = jnp.concatenate([accs[v], accs[v + 1]], axis=0)
                o_ref[rows, lanes] = (gate_ref[rows, lanes].astype(F32) * acc).astype(o_ref.dtype)
        return carry

    lax.fori_loop(0, (tm // row_chunk) * n_lane_chunks, chunk, 0)
    if mode == "ln_silu":
        o_ref[...] = _silu(_layer_norm(y_ref[...], g_ref[...], beta_ref[...])).astype(o_ref.dtype)


def _causal_conv(x, w, seq, tm, mode, extra, name):
    t, c = x.shape
    taps = w.shape[0]
    assert taps - 1 <= HALO_ROWS and tm % HALO_ROWS == 0
    tiles_per_seq = seq // tm
    halo_blocks = tm // HALO_ROWS
    row = lambda i: (0, 0)
    in_specs = [
        pl.BlockSpec((tm, c), lambda i: (i, 0)),
        pl.BlockSpec((HALO_ROWS, c), lambda i: (jnp.maximum(i * halo_blocks - 1, 0), 0)),
        pl.BlockSpec((taps, c), row),
    ]
    _, residues = _conv_shifts(taps)
    scratch = [pltpu.VMEM((HALO_ROWS + tm, c), F32),
               pltpu.VMEM((max(len(residues), 1), HALO_ROWS + tm - SUBLANES, c), F32)]
    if mode == "ln_silu":
        in_specs += [pl.BlockSpec((1, c), row)] * 3
        args = (x, x, w) + tuple(v.reshape(1, c) for v in extra)
        scratch.append(pltpu.VMEM((tm, c), F32))
    else:
        in_specs.append(pl.BlockSpec((tm, c), lambda i: (i, 0)))
        args = (x, x, w) + tuple(extra)
    return pl.pallas_call(
        functools.partial(_conv_kernel, taps=taps, tm=tm, tiles_per_seq=tiles_per_seq, mode=mode),
        grid=(t // tm,),
        in_specs=in_specs,
        out_specs=pl.BlockSpec((tm, c), lambda i: (i, 0)),
        out_shape=jax.ShapeDtypeStruct((t, c), BF16),
        scratch_shapes=scratch,
        compiler_params=_params(1),
        name=name,
    )(*args)


def _residual_ln(x, y, mod_ref, lng_ref, lnb_ref, alpha, gate_row):
    z = alpha * x + mod_ref[gate_row:gate_row + 1, :] * y
    return _layer_norm(z, lng_ref[...], lnb_ref[...])


def _top2_of4(a, b, c, d):
    hi1, lo1 = jnp.maximum(a, b), jnp.minimum(a, b)
    hi2, lo2 = jnp.maximum(c, d), jnp.minimum(c, d)
    return jnp.maximum(hi1, hi2) + jnp.maximum(jnp.minimum(hi1, hi2), jnp.maximum(lo1, lo2))


def _first_argmax(vals):
    idx = jnp.zeros(vals[0].shape, I32)
    best = vals[0]
    for j in range(1, len(vals)):
        take = vals[j] > best
        idx = jnp.where(take, j, idx)
        best = jnp.where(take, vals[j], best)
    return idx


def _select(rows, idx):
    out = rows[0]
    for j in range(1, len(rows)):
        out = jnp.where(idx == j, rows[j], out)
    return out


def _ln_router_kernel(x_ref, y_ref, mod_ref, lng_ref, lnb_ref, wrh_ref, wrl_ref, rb_ref, tri_ref,
                      xo_ref, h_ref, ri_ref, rw_ref, cnt_ref, carry_ref, *, alpha, n_experts):
    tm = x_ref.shape[0]
    per_group = n_experts // N_GROUPS

    @pl.when(pl.program_id(0) == 0)
    def _init():
        carry_ref[...] = jnp.zeros_like(carry_ref)

    xn = _residual_ln(x_ref[...], y_ref[...].astype(F32), mod_ref, lng_ref, lnb_ref, alpha, 2)
    xo_ref[...] = xn
    h = xn * (1.0 + mod_ref[4:5, :]) + mod_ref[3:4, :]
    h_hi = h.astype(BF16)
    h_ref[...] = h.reshape(h_ref.shape).astype(BF16)
    h_lo = (h - h_hi.astype(F32)).astype(BF16)
    logits = (jnp.dot(h_hi, wrh_ref[...], preferred_element_type=F32)
              + jnp.dot(h_lo, wrh_ref[...], preferred_element_type=F32)
              + jnp.dot(h_hi, wrl_ref[...], preferred_element_type=F32))
    lg = logits.T[0:n_experts, :]
    mx = jnp.max(lg, axis=0, keepdims=True)
    ex = jnp.exp(lg - mx)
    probs = ex / jnp.sum(ex, axis=0, keepdims=True)
    sel = probs + rb_ref[...]
    sel_rows = [sel[e:e + 1, :] for e in range(n_experts)]
    prob_rows = [probs[e:e + 1, :] for e in range(n_experts)]
    assert per_group == 4
    group_scores = [_top2_of4(*sel_rows[per_group * g:per_group * (g + 1)]) for g in range(N_GROUPS)]
    best_group = _first_argmax(group_scores)
    v = [_select([sel_rows[per_group * g + j] for g in range(N_GROUPS)], best_group) for j in range(per_group)]
    p = [_select([prob_rows[per_group * g + j] for g in range(N_GROUPS)], best_group) for j in range(per_group)]
    i1 = _first_argmax(v)
    i2 = _first_argmax([jnp.where(i1 == j, -jnp.inf, v[j]) for j in range(per_group)])
    p1, p2 = _select(p, i1), _select(p, i2)
    e1 = best_group * per_group + i1
    e2 = best_group * per_group + i2
    eid = lax.broadcasted_iota(I32, (n_experts, tm), 0)
    onehot = jnp.where((eid == e1) | (eid == e2), 1.0, 0.0)
    prefix = jnp.dot(onehot.astype(BF16), tri_ref[...], preferred_element_type=F32)
    tot = carry_ref[:, 0:1] + prefix
    rank1 = jnp.sum(jnp.where(eid == e1, tot, 0.0), axis=0, keepdims=True)
    rank2 = jnp.sum(jnp.where(eid == e2, tot, 0.0), axis=0, keepdims=True)
    new_carry = carry_ref[...] + jnp.sum(onehot, axis=1, keepdims=True)
    carry_ref[...] = new_carry
    cnt_ref[...] = new_carry.astype(I32)
    ri_ref[0:1, :] = e1
    ri_ref[1:2, :] = e2
    ri_ref[2:3, :] = rank1.astype(I32)
    ri_ref[3:4, :] = rank2.astype(I32)
    ri_ref[4:8, :] = jnp.zeros((4, tm), I32)
    rowid = lax.broadcasted_iota(I32, (LANES, tm), 0)
    wsum = p1 + p2
    slab = jnp.where(rowid == 0, p1 / wsum, jnp.where(rowid == 1, p2 / wsum, 0.0))
    rw_ref[...] = slab.T


def _ln_router(x, y, mod, layer, lng, lnb, w_router, router_bias, seq, alpha, tm):
    t, d = x.shape
    n_experts = w_router.shape[1]
    tiles_per_seq = seq // tm
    wr = jnp.zeros((d, LANES), F32).at[:, :n_experts].set(w_router)
    wr_hi = wr.astype(BF16)
    wr_lo = (wr - wr_hi.astype(F32)).astype(BF16)
    rb = jnp.broadcast_to(router_bias.astype(F32)[:, None], (n_experts, tm))
    tri = (lax.broadcasted_iota(I32, (tm, tm), 0) < lax.broadcasted_iota(I32, (tm, tm), 1)).astype(BF16)
    tok = lambda i: (i, 0)
    const = lambda i: (0, 0)
    return pl.pallas_call(
        functools.partial(_ln_router_kernel, alpha=alpha, n_experts=n_experts),
        grid=(t // tm,),
        in_specs=[
            pl.BlockSpec((tm, d), tok),
            pl.BlockSpec((tm, d), tok),
            pl.BlockSpec((None, None, N_MOD, d), lambda i: (layer, i // tiles_per_seq, 0, 0)),
            pl.BlockSpec((1, d), const),
            pl.BlockSpec((1, d), const),
            pl.BlockSpec((d, LANES), const),
            pl.BlockSpec((d, LANES), const),
            pl.BlockSpec((n_experts, tm), const),
            pl.BlockSpec((tm, tm), const),
        ],
        out_specs=[
            pl.BlockSpec((tm, d), tok),
            pl.BlockSpec((tm, d // LANES, LANES), lambda i: (i, 0, 0)),
            pl.BlockSpec((SUBLANES, tm), lambda i: (0, i)),
            pl.BlockSpec((tm, LANES), tok),
            pl.BlockSpec((n_experts, LANES), const),
        ],
        out_shape=[
            jax.ShapeDtypeStruct((t, d), F32),
            jax.ShapeDtypeStruct((t, d // LANES, LANES), BF16),
            jax.ShapeDtypeStruct((SUBLANES, t), I32),
            jax.ShapeDtypeStruct((t, LANES), F32),
            jax.ShapeDtypeStruct((n_experts, LANES), I32),
        ],
        scratch_shapes=[pltpu.VMEM((n_experts, LANES), F32)],
        compiler_params=_params(1),
        name="ln_router",
    )(x, y, mod, lng.reshape(1, d), lnb.reshape(1, d), wr_hi, wr_lo, rb, tri)


def _row_copy(src_ref, src_row, dst_ref, dst_row, sem):
    return pltpu.make_async_copy(src_ref.at[pl.ds(src_row, 1)], dst_ref.at[pl.ds(dst_row, 1)], sem)


def _dispatch_kernel(pos1_ref, pos2_ref, h_ref, xs_in_ref, xs_ref, sem, *, tm):
    del xs_in_ref
    base = pl.program_id(0) * tm

    def start(t, carry):
        _row_copy(h_ref, t, xs_ref, pos1_ref[base + t], sem).start()
        _row_copy(h_ref, t, xs_ref, pos2_ref[base + t], sem).start()
        return carry

    lax.fori_loop(0, tm, start, 0, unroll=8)
    for _ in range(2):
        pltpu.make_async_copy(h_ref, xs_ref.at[pl.ds(0, tm)], sem).wait()


def _dispatch(h, pos1, pos2, xs0, tm):
    t = h.shape[0]
    return pl.pallas_call(
        functools.partial(_dispatch_kernel, tm=tm),
        grid_spec=pltpu.PrefetchScalarGridSpec(
            num_scalar_prefetch=2,
            grid=(t // tm,),
            in_specs=[pl.BlockSpec((tm,) + h.shape[1:], lambda i, p1, p2: (i, 0, 0)),
                      pl.BlockSpec(memory_space=pl.ANY)],
            out_specs=pl.BlockSpec(memory_space=pl.ANY),
            scratch_shapes=[pltpu.SemaphoreType.DMA(())],
        ),
        out_shape=jax.ShapeDtypeStruct(xs0.shape, h.dtype),
        input_output_aliases={3: 0},
        compiler_params=_params(1),
        name="moe_dispatch",
    )(pos1, pos2, h, xs0)


def _first_tile_of_expert(te_ref, j):
    return (j == 0) | (te_ref[j] != te_ref[jnp.maximum(j - 1, 0)])


def _expert_weight_copies(w_refs, layer, expert, col0, stage_ref, sems):
    width = stage_ref.shape[2]
    return [pltpu.make_async_copy(w.at[layer, expert].at[:, pl.ds(col0, width)], stage_ref.at[i], sems.at[i])
            for i, w in enumerate(w_refs)]


def _stream_expert_weights(te_ref, nxt_ref, w_refs, layer, stage_ref, wb_ref, sems):
    p, j = pl.program_id(0), pl.program_id(1)
    width = stage_ref.shape[2]

    def copies(expert, col_pass):
        return _expert_weight_copies(w_refs, layer, expert, pl.multiple_of(col_pass * width, width),
                                     stage_ref, sems)

    @pl.when(_first_tile_of_expert(te_ref, j))
    def _switch_expert():
        @pl.when((p == 0) & (j == 0))
        def _very_first():
            for c in copies(te_ref[0], 0):
                c.start()

        for c in copies(te_ref[j], p):
            c.wait()
        for i in range(len(w_refs)):
            wb_ref[i] = stage_ref[i].astype(BF16)
        nxt = nxt_ref[j]

        @pl.when(nxt >= 0)
        def _next_expert():
            for c in copies(nxt, p):
                c.start()

        @pl.when((nxt < 0) & (p + 1 < pl.num_programs(0)))
        def _next_pass():
            for c in copies(te_ref[0], p + 1):
                c.start()


def _moe_up_kernel(te_ref, xt_ref, rows_ref, nxt_ref, x_ref, wg_ref, wu_ref, o_ref, stage_ref, wb_ref, sems,
                   *, layer):
    del xt_ref
    j = pl.program_id(1)
    half = x_ref.shape[0] // 2
    _stream_expert_weights(te_ref, nxt_ref, (wg_ref, wu_ref), layer, stage_ref, wb_ref, sems)

    def half_tile(r0, used):
        @pl.when(used)
        def _compute():
            x = x_ref[r0:r0 + half].reshape(half, wb_ref.shape[1])
            gate = jnp.dot(x, wb_ref[0], preferred_element_type=F32)
            up = jnp.dot(x, wb_ref[1], preferred_element_type=F32)
            o_ref[r0:r0 + half, :] = (_silu(gate) * up).astype(o_ref.dtype)

        @pl.when(jnp.logical_not(used))
        def _unused():
            o_ref[r0:r0 + half, :] = jnp.zeros((half, o_ref.shape[1]), o_ref.dtype)

    half_tile(0, rows_ref[j] > 0)
    half_tile(half, rows_ref[j] > half)


def _moe_up(xs, w_gate, w_up, layer, te, xt, rows, nxt, tm, bf):
    n_rows = xs.shape[0]
    d, f = w_gate.shape[2], w_gate.shape[3]
    n_tiles = n_rows // tm
    return pl.pallas_call(
        functools.partial(_moe_up_kernel, layer=layer),
        grid_spec=pltpu.PrefetchScalarGridSpec(
            num_scalar_prefetch=4,
            grid=(f // bf, n_tiles),
            in_specs=[pl.BlockSpec((tm, d // LANES, LANES), lambda jf, j, te, xt, rows, nxt: (xt[j], 0, 0)),
                      pl.BlockSpec(memory_space=pl.ANY), pl.BlockSpec(memory_space=pl.ANY)],
            out_specs=pl.BlockSpec((tm, bf), lambda jf, j, te, xt, rows, nxt: (j, jf)),
            scratch_shapes=[pltpu.VMEM((2, d, bf), F32), pltpu.VMEM((2, d, bf), BF16),
                            pltpu.SemaphoreType.DMA((2,))],
        ),
        out_shape=jax.ShapeDtypeStruct((n_rows, f), BF16),
        compiler_params=_params(2),
        name="moe_up",
    )(te, xt, rows, nxt, xs, w_gate, w_up)


def _moe_down_kernel(te_ref, xt_ref, rows_ref, nxt_ref, x_ref, w_ref, o_ref, stage_ref, wb_ref, sems, *, layer):
    del xt_ref
    j = pl.program_id(1)
    half = x_ref.shape[0] // 2
    _stream_expert_weights(te_ref, nxt_ref, (w_ref,), layer, stage_ref, wb_ref, sems)

    def half_tile(r0, used):
        @pl.when(used)
        def _compute():
            acc = jnp.dot(x_ref[r0:r0 + half, :], wb_ref[0], preferred_element_type=F32)
            o_ref[r0:r0 + half] = acc.reshape((half,) + o_ref.shape[1:]).astype(o_ref.dtype)

        @pl.when(jnp.logical_not(used))
        def _unused():
            o_ref[r0:r0 + half] = jnp.zeros((half,) + o_ref.shape[1:], o_ref.dtype)

    half_tile(0, rows_ref[j] > 0)
    half_tile(half, rows_ref[j] > half)


def _moe_down(hid, w_down, layer, te, xt, rows, nxt, tm, bn):
    n_rows, f = hid.shape
    d = w_down.shape[3]
    n_tiles = n_rows // tm
    return pl.pallas_call(
        functools.partial(_moe_down_kernel, layer=layer),
        grid_spec=pltpu.PrefetchScalarGridSpec(
            num_scalar_prefetch=4,
            grid=(d // bn, n_tiles),
            in_specs=[
                pl.BlockSpec((tm, f), lambda jn, j, te, xt, rows, nxt: (xt[j], 0)),
                pl.BlockSpec(memory_space=pl.ANY),
            ],
            out_specs=pl.BlockSpec((tm, bn // LANES, LANES), lambda jn, j, te, xt, rows, nxt: (j, jn, 0)),
            scratch_shapes=[pltpu.VMEM((1, f, bn), F32), pltpu.VMEM((1, f, bn), BF16),
                            pltpu.SemaphoreType.DMA((1,))],
        ),
        out_shape=jax.ShapeDtypeStruct((n_rows, d // LANES, LANES), BF16),
        compiler_params=_params(2),
        name="moe_down",
    )(te, xt, rows, nxt, hid, w_down)


def _combine_ln_kernel(pos1_ref, pos2_ref, x_ref, rw_ref, mod_ref, lng_ref, lnb_ref, nmod_ref, ys_ref,
                       *rest, alpha, emit_h):
    if emit_h:
        xo_ref, h_ref, buf_ref, sem = rest
    else:
        xo_ref, buf_ref, sem = rest
    tm = x_ref.shape[0]
    i = pl.program_id(0)

    def gather(tile, slot):
        base = tile * tm

        def start(t, carry):
            _row_copy(ys_ref, pos1_ref[base + t], buf_ref.at[slot, 0], t, sem.at[slot]).start()
            _row_copy(ys_ref, pos2_ref[base + t], buf_ref.at[slot, 1], t, sem.at[slot]).start()
            return carry

        lax.fori_loop(0, tm, start, 0, unroll=8)

    @pl.when(i == 0)
    def _first_tile():
        gather(0, 0)

    @pl.when(i + 1 < pl.num_programs(0))
    def _prefetch():
        gather(i + 1, (i + 1) % 2)

    slot = i % 2
    for k in range(2):
        pltpu.make_async_copy(ys_ref.at[pl.ds(0, tm)], buf_ref.at[slot, k], sem.at[slot]).wait()
    rw = rw_ref[...]
    y1 = buf_ref[slot, 0].reshape(x_ref.shape).astype(F32)
    y2 = buf_ref[slot, 1].reshape(x_ref.shape).astype(F32)
    y = rw[:, 0:1] * y1 + rw[:, 1:2] * y2
    xn = _residual_ln(x_ref[...], y, mod_ref, lng_ref, lnb_ref, alpha, 5)
    xo_ref[...] = xn
    if emit_h:
        h_ref[...] = (xn * (1.0 + nmod_ref[1:2, :]) + nmod_ref[0:1, :]).astype(BF16)


def _combine_ln(x, ys, pos1, pos2, rw, mod, layer, next_layer, lng, lnb, seq, alpha, tm):
    t, d = x.shape
    tiles_per_seq = seq // tm
    emit_h = next_layer is not None
    tok = lambda i, p1, p2: (i, 0)
    const = lambda i, p1, p2: (0, 0)
    mod_spec = lambda l: pl.BlockSpec((None, None, N_MOD, d), lambda i, p1, p2: (l, i // tiles_per_seq, 0, 0))
    out_specs = [pl.BlockSpec((tm, d), tok)]
    out_shape = [jax.ShapeDtypeStruct((t, d), F32)]
    if emit_h:
        out_specs.append(pl.BlockSpec((tm, d), tok))
        out_shape.append(jax.ShapeDtypeStruct((t, d), BF16))
    outs = pl.pallas_call(
        functools.partial(_combine_ln_kernel, alpha=alpha, emit_h=emit_h),
        grid_spec=pltpu.PrefetchScalarGridSpec(
            num_scalar_prefetch=2,
            grid=(t // tm,),
            in_specs=[
                pl.BlockSpec((tm, d), tok),
                pl.BlockSpec((tm, LANES), tok),
                mod_spec(layer),
                pl.BlockSpec((1, d), const),
                pl.BlockSpec((1, d), const),
                mod_spec(next_layer if emit_h else layer),
                pl.BlockSpec(memory_space=pl.ANY),
            ],
            out_specs=out_specs,
            scratch_shapes=[pltpu.VMEM((2, 2, tm, d // LANES, LANES), BF16), pltpu.SemaphoreType.DMA((2,))],
        ),
        out_shape=out_shape,
        compiler_params=_params(1),
        name="moe_combine_ln",
    )(pos1, pos2, x, rw, mod, lng.reshape(1, d), lnb.reshape(1, d), mod, ys)
    return (outs[0], outs[1]) if emit_h else (outs[0], None)


def _tiles(t, seq, d, d_ff):
    return dict(
        bm=min(1024, seq),
        bn1=min(1024, d // 2),
        bn2=min(256, d // 2),
        bn_merge=min(512, d // 2),
        mod_bn=min(1024, d),
        conv_tm=min(256, seq),
        ln_tm=min(256, seq),
        moe_tm=min(512, seq),
        moe_bf=min(512, d_ff),
        moe_bn=min(4096, d),
        dispatch_tm=min(512, seq),
    )


def _moe_schedule(counts, ranks, experts, tm, n_tiles):
    n_experts = counts.shape[0]
    tiles_per_expert = (counts + tm - 1) // tm
    tile_end = jnp.cumsum(tiles_per_expert)
    tile_start = tile_end - tiles_per_expert
    total = tile_end[-1]
    row_start = tile_start * tm
    pos = [row_start[experts[k]] + ranks[k] for k in range(2)]
    j = jnp.arange(n_tiles, dtype=I32)
    xt = jnp.minimum(j, total - 1)
    te = jnp.minimum(jnp.sum((xt[:, None] >= tile_end[None, :]).astype(I32), axis=1), n_experts - 1)
    rows = jnp.where(j < total, jnp.clip(counts[te] - (j - tile_start[te]) * tm, 0, tm), 0)
    eid = jnp.arange(n_experts, dtype=I32)
    later = jnp.where((eid[None, :] > eid[:, None]) & (tiles_per_expert[None, :] > 0), eid[None, :], n_experts)
    next_expert = jnp.min(later, axis=1)
    nxt = jnp.where(next_expert < n_experts, next_expert, -1)[te]
    return (pos[0].astype(I32), pos[1].astype(I32), te.astype(I32), xt.astype(I32), rows.astype(I32),
            nxt.astype(I32))


def kernel(x, c, w_ada, b_ada, w_in, conv_a_w, conv_a_b, ln_a_g, ln_a_b, conv_b_w, w_pa, w_pb, w_o,
           ln_mix_g, ln_mix_b, w_router, router_bias, w_gate, w_up, w_down, ln_ffn_g, ln_ffn_b):
    batch, seq, d = x.shape
    n_layers = w_in.shape[0]
    d_conf = conv_a_w.shape[2]
    d_short = conv_b_w.shape[2]
    n_experts, d_ff = w_gate.shape[1], w_gate.shape[3]
    t = batch * seq
    alpha = float((2 * n_layers) ** 0.25)
    ts = _tiles(t, seq, d, d_ff)
    bm, bn1, bn2 = ts["bm"], ts["bn1"], ts["bn2"]
    c_aval, c_agate = 0, d_conf
    c_sb, c_sc, c_sx = 2 * d_conf, 2 * d_conf + d_short, 2 * d_conf + 2 * d_short
    c_ga = 2 * d_conf + 3 * d_short
    n_moe_tiles = (2 * t) // ts["moe_tm"] + n_experts
    n_moe_rows = n_moe_tiles * ts["moe_tm"]

    mod = _modulation(c, w_ada, b_ada, ts["mod_bn"])
    xc = x.reshape(t, d)
    h = _modulate(xc, mod, 0, seq, ts["ln_tm"])
    xs = jnp.zeros((n_moe_rows, d // LANES, LANES), BF16)
    for l in range(n_layers):
        w_in_l = lambda col0: (w_in, l, col0, 0)
        ya_pre = _matmul([h], [w_in_l(c_aval), w_in_l(c_agate)], [], _ep_glu, d_conf, F32, bm, bn2, "inproj_glu")
        cx = _matmul([h], [w_in_l(c_sc), w_in_l(c_sx)], [], _ep_mul, d_short, F32, bm, bn2, "inproj_cx")
        sb = _matmul([h], [w_in_l(c_sb)], [], _ep_id, d_short, BF16, bm, bn1, "inproj_sb")
        gates = _matmul([h], [w_in_l(c_ga)], [], _ep_sigmoid, 2 * d, BF16, bm, bn1, "inproj_gates")
        ya = _causal_conv(ya_pre, conv_a_w[l], seq, ts["conv_tm"], "ln_silu",
                          (conv_a_b[l], ln_a_g[l], ln_a_b[l]), "conv_a")
        yb = _causal_conv(cx, conv_b_w[l], seq, ts["conv_tm"], "gate", (sb,), "conv_b")
        m = _matmul([ya, yb], [(w_pa, l, 0, 0), (w_pb, l, 0, 1)], [(gates, 0), (gates, d)],
                    _ep_merge, d, BF16, bm, ts["bn_merge"], "proj_merge")
        y = _matmul([m], [(w_o, l, 0, 0)], [], _ep_id, d, BF16, bm, bn1, "out_proj")
        xc, h2, ri, rw, cnt = _ln_router(xc, y, mod, l, ln_mix_g[l], ln_mix_b[l], w_router, router_bias,
                                         seq, alpha, ts["ln_tm"])
        pos1, pos2, te, xt, rows, nxt = _moe_schedule(cnt[:, 0], (ri[2], ri[3]), (ri[0], ri[1]),
                                                      ts["moe_tm"], n_moe_tiles)
        xs = _dispatch(h2, pos1, pos2, xs, ts["dispatch_tm"])
        hid = _moe_up(xs, w_gate, w_up, l, te, xt, rows, nxt, ts["moe_tm"], ts["moe_bf"])
        ys = _moe_down(hid, w_down, l, te, xt, rows, nxt, ts["moe_tm"], ts["moe_bn"])
        xc, h = _combine_ln(xc, ys, pos1, pos2, rw, mod, l, l + 1 if l + 1 < n_layers else None,
                            ln_ffn_g[l], ln_ffn_b[l], seq, alpha, ts["ln_tm"])
    return xc.reshape(batch, seq, d)
```

```python
import functools

import jax
import jax.numpy as jnp
from jax import lax
from jax.experimental import pallas as pl
from jax.experimental.pallas import tpu as pltpu

F32 = jnp.float32
BF16 = jnp.bfloat16
I32 = jnp.int32

LN_EPS = 1e-5
N_GROUPS = 4
N_MOD = 6
V7X_VMEM_LIMIT_BYTES = 60000 * 1024
LANES = 128
SUBLANES = 8
HALO_ROWS = 32
MM_ROW_BLOCK = 256
CONV_ROW_CHUNK = 128


def _params(n_grid):
    return pltpu.CompilerParams(
        dimension_semantics=("arbitrary",) * n_grid, vmem_limit_bytes=V7X_VMEM_LIMIT_BYTES)


def _sigmoid(v):
    return 1.0 / (1.0 + jnp.exp(-v))


def _silu(v):
    return v * _sigmoid(v)


def _layer_norm(z, g, b):
    mu = jnp.mean(z, axis=-1, keepdims=True)
    zc = z - mu
    var = jnp.mean(zc * zc, axis=-1, keepdims=True)
    return zc * lax.rsqrt(var + LN_EPS) * g + b


def _mod_kernel(c_ref, w_ref, b_ref, o_ref):
    cond = _silu(c_ref[...])
    acc = jnp.dot(cond.astype(BF16), w_ref[...].astype(BF16), preferred_element_type=F32)
    o_ref[...] = acc + b_ref[...]


def _modulation(c, w_ada, b_ada, bn):
    n_layers, d, n = w_ada.shape
    b = c.shape[0]
    c_pad = jnp.zeros((SUBLANES, d), F32).at[:b].set(c)
    out = pl.pallas_call(
        _mod_kernel,
        grid=(n_layers, n // bn),
        in_specs=[
            pl.BlockSpec((SUBLANES, d), lambda l, j: (0, 0)),
            pl.BlockSpec((None, d, bn), lambda l, j: (l, 0, j)),
            pl.BlockSpec((None, 1, bn), lambda l, j: (l, 0, j)),
        ],
        out_specs=pl.BlockSpec((None, SUBLANES, bn), lambda l, j: (l, 0, j)),
        out_shape=jax.ShapeDtypeStruct((n_layers, SUBLANES, n), F32),
        compiler_params=_params(2),
        name="adaln_mod",
    )(c_pad, w_ada, b_ada.reshape(n_layers, 1, n))
    return out[:, :b].reshape(n_layers, b, N_MOD, d)


def _modulate_kernel(x_ref, mod_ref, h_ref, *, shift_row, scale_row):
    sc = mod_ref[scale_row:scale_row + 1, :]
    sh = mod_ref[shift_row:shift_row + 1, :]
    h_ref[...] = (x_ref[...] * (1.0 + sc) + sh).astype(BF16)


def _modulate(x2, mod, layer, seq, tm):
    t, d = x2.shape
    tiles_per_seq = seq // tm
    return pl.pallas_call(
        functools.partial(_modulate_kernel, shift_row=0, scale_row=1),
        grid=(t // tm,),
        in_specs=[
            pl.BlockSpec((tm, d), lambda i: (i, 0)),
            pl.BlockSpec((None, None, N_MOD, d), lambda i: (layer, i // tiles_per_seq, 0, 0)),
        ],
        out_specs=pl.BlockSpec((tm, d), lambda i: (i, 0)),
        out_shape=jax.ShapeDtypeStruct((t, d), BF16),
        compiler_params=_params(1),
        name="modulate_in",
    )(x2, mod)


def _mm_kernel(*refs, n_x, w_cols, w_x, n_e, epilogue):
    n_w = len(w_x)
    x_refs = refs[:n_x]
    w_refs = refs[n_x:n_x + n_w]
    e_refs = refs[n_x + n_w:n_x + n_w + n_e]
    o_ref, stage_ref, wb_ref, sems = refs[n_x + n_w + n_e:]
    j, i = pl.program_id(0), pl.program_id(1)
    bn = wb_ref.shape[2]

    def copies(col_tile):
        return [pltpu.make_async_copy(
            w.at[lead].at[:, pl.ds(pl.multiple_of((cb + col_tile) * bn, bn), bn)], stage_ref.at[k], sems.at[k])
            for k, (w, (lead, cb)) in enumerate(zip(w_refs, w_cols))]

    @pl.when(i == 0)
    def _next_column_tile():
        @pl.when(j == 0)
        def _very_first():
            for c in copies(0):
                c.start()

        for c in copies(j):
            c.wait()
        for k in range(n_w):
            wb_ref[k] = stage_ref[k].astype(BF16)

        @pl.when(j + 1 < pl.num_programs(0))
        def _prefetch():
            for c in copies(j + 1):
                c.start()

    rows = min(MM_ROW_BLOCK, o_ref.shape[0])
    for r0 in range(0, o_ref.shape[0], rows):
        accs = [jnp.dot(x_refs[w_x[k]][r0:r0 + rows, :], wb_ref[k], preferred_element_type=F32)
                for k in range(n_w)]
        extras = [e[r0:r0 + rows, :] for e in e_refs]
        o_ref[r0:r0 + rows, :] = epilogue(accs, extras).astype(o_ref.dtype)


def _matmul(xs, ws, extras, epilogue, n_out, out_dtype, bm, bn, name):
    m, k = xs[0].shape
    n_w = len(ws)
    in_specs = [pl.BlockSpec((bm, x.shape[1]), lambda j, i: (i, 0)) for x in xs]
    args = list(xs)
    for w, lead, col0, _ in ws:
        assert col0 % bn == 0 and w.shape[1] == k
        in_specs.append(pl.BlockSpec(memory_space=pl.ANY))
        args.append(w)
    for e, col0 in extras:
        assert col0 % bn == 0
        in_specs.append(pl.BlockSpec((bm, bn), functools.partial(
            lambda j, i, cb: (i, cb + j), cb=col0 // bn)))
        args.append(e)
    return pl.pallas_call(
        functools.partial(_mm_kernel, n_x=len(xs), w_cols=tuple((w[1], w[2] // bn) for w in ws),
                          w_x=tuple(w[3] for w in ws), n_e=len(extras), epilogue=epilogue),
        grid=(n_out // bn, m // bm),
        in_specs=in_specs,
        out_specs=pl.BlockSpec((bm, bn), lambda j, i: (i, j)),
        out_shape=jax.ShapeDtypeStruct((m, n_out), out_dtype),
        scratch_shapes=[pltpu.VMEM((n_w, k, bn), F32), pltpu.VMEM((n_w, k, bn), BF16),
                        pltpu.SemaphoreType.DMA((n_w,))],
        compiler_params=_params(2),
        name=name,
    )(*args)


def _ep_glu(accs, _):
    return accs[0] * _sigmoid(accs[1])


def _ep_mul(accs, _):
    return accs[0] * accs[1]


def _ep_id(accs, _):
    return accs[0]


def _ep_sigmoid(accs, _):
    return _sigmoid(accs[0])


def _ep_merge(accs, extras):
    return extras[0].astype(F32) * accs[0] + extras[1].astype(F32) * accs[1]


def _conv_shifts(taps):
    starts = [HALO_ROWS - (taps - 1) + k for k in range(taps)]
    return starts, sorted({s % SUBLANES for s in starts} - {0})


def _conv_kernel(*refs, taps, tm, tiles_per_seq, mode):
    if mode == "ln_silu":
        cur_ref, halo_ref, w_ref, b_ref, g_ref, beta_ref, o_ref, buf_ref, sh_ref, y_ref = refs
    else:
        cur_ref, halo_ref, w_ref, gate_ref, o_ref, buf_ref, sh_ref = refs
    c = cur_ref.shape[1]
    first = (pl.program_id(0) % tiles_per_seq) == 0
    buf_ref[0:HALO_ROWS, :] = jnp.where(first, 0.0, halo_ref[...])
    buf_ref[HALO_ROWS:HALO_ROWS + tm, :] = cur_ref[...]
    starts, residues = _conv_shifts(taps)
    sh_rows = sh_ref.shape[1]
    for n, r in enumerate(residues):
        sh_ref[n] = buf_ref[r:r + sh_rows, :]
    row_chunk = min(CONV_ROW_CHUNK, tm)
    n_vregs = row_chunk // SUBLANES
    n_lane_chunks = c // LANES

    def chunk(idx, carry):
        r0 = pl.multiple_of((idx // n_lane_chunks) * row_chunk, row_chunk)
        lanes = pl.ds(pl.multiple_of((idx % n_lane_chunks) * LANES, LANES), LANES)
        accs = [jnp.zeros((SUBLANES, LANES), F32)] * n_vregs
        for kk in range(taps):
            q, r = divmod(starts[kk], SUBLANES)
            wk = jnp.broadcast_to(w_ref[kk:kk + 1, lanes], (SUBLANES, LANES))
            src = buf_ref if r == 0 else sh_ref.at[residues.index(r)]
            for v in range(n_vregs):
                rows = pl.ds(pl.multiple_of(r0 + (q + v) * SUBLANES, SUBLANES), SUBLANES)
                accs[v] = accs[v] + wk * src[rows, lanes]
        if mode == "ln_silu":
            bias = jnp.broadcast_to(b_ref[:, lanes], (SUBLANES, LANES))
            for v in range(n_vregs):
                y_ref[pl.ds(pl.multiple_of(r0 + v * SUBLANES, SUBLANES), SUBLANES), lanes] = accs[v] + bias
        else:
            for v in range(0, n_vregs, 2):
                rows = pl.ds(pl.multiple_of(r0 + v * SUBLANES, 2 * SUBLANES), 2 * SUBLANES)
                acc = jnp.concatenate([accs[v], accs[v + 1]], axis=0)
                o_ref[rows, lanes] = (gate_ref[rows, lanes].astype(F32) * acc).astype(o_ref.dtype)
        return carry

    lax.fori_loop(0, (tm // row_chunk) * n_lane_chunks, chunk, 0)
    if mode == "ln_silu":
        o_ref[...] = _silu(_layer_norm(y_ref[...], g_ref[...], beta_ref[...])).astype(o_ref.dtype)


def _causal_conv(x, w, seq, tm, mode, extra, name):
    t, c = x.shape
    taps = w.shape[0]
    assert taps - 1 <= HALO_ROWS and tm % HALO_ROWS == 0
    tiles_per_seq = seq // tm
    halo_blocks = tm // HALO_ROWS
    row = lambda i: (0, 0)
    in_specs = [
        pl.BlockSpec((tm, c), lambda i: (i, 0)),
        pl.BlockSpec((HALO_ROWS, c), lambda i: (jnp.maximum(i * halo_blocks - 1, 0), 0)),
        pl.BlockSpec((taps, c), row),
    ]
    _, residues = _conv_shifts(taps)
    scratch = [pltpu.VMEM((HALO_ROWS + tm, c), F32),
               pltpu.VMEM((max(len(residues), 1), HALO_ROWS + tm - SUBLANES, c), F32)]
    if mode == "ln_silu":
        in_specs += [pl.BlockSpec((1, c), row)] * 3
        args = (x, x, w) + tuple(v.reshape(1, c) for v in extra)
        scratch.append(pltpu.VMEM((tm, c), F32))
    else:
        in_specs.append(pl.BlockSpec((tm, c), lambda i: (i, 0)))
        args = (x, x, w) + tuple(extra)
    return pl.pallas_call(
        functools.partial(_conv_kernel, taps=taps, tm=tm, tiles_per_seq=tiles_per_seq, mode=mode),
        grid=(t // tm,),
        in_specs=in_specs,
        out_specs=pl.BlockSpec((tm, c), lambda i: (i, 0)),
        out_shape=jax.ShapeDtypeStruct((t, c), BF16),
        scratch_shapes=scratch,
        compiler_params=_params(1),
        name=name,
    )(*args)


def _residual_ln(x, y, mod_ref, lng_ref, lnb_ref, alpha, gate_row):
    z = alpha * x + mod_ref[gate_row:gate_row + 1, :] * y
    return _layer_norm(z, lng_ref[...], lnb_ref[...])


def _top2_of4(a, b, c, d):
    hi1, lo1 = jnp.maximum(a, b), jnp.minimum(a, b)
    hi2, lo2 = jnp.maximum(c, d), jnp.minimum(c, d)
    return jnp.maximum(hi1, hi2) + jnp.maximum(jnp.minimum(hi1, hi2), jnp.maximum(lo1, lo2))


def _first_argmax(vals):
    idx = jnp.zeros(vals[0].shape, I32)
    best = vals[0]
    for j in range(1, len(vals)):
        take = vals[j] > best
        idx = jnp.where(take, j, idx)
        best = jnp.where(take, vals[j], best)
    return idx


def _select(rows, idx):
    out = rows[0]
    for j in range(1, len(rows)):
        out = jnp.where(idx == j, rows[j], out)
    return out


def _ln_router_kernel(x_ref, y_ref, mod_ref, lng_ref, lnb_ref, wrh_ref, wrl_ref, rb_ref, tri_ref,
                      xo_ref, h_ref, ri_ref, rw_ref, cnt_ref, carry_ref, *, alpha, n_experts):
    tm = x_ref.shape[0]
    per_group = n_experts // N_GROUPS

    @pl.when(pl.program_id(0) == 0)
    def _init():
        carry_ref[...] = jnp.zeros_like(carry_ref)

    xn = _residual_ln(x_ref[...], y_ref[...].astype(F32), mod_ref, lng_ref, lnb_ref, alpha, 2)
    xo_ref[...] = xn
    h = xn * (1.0 + mod_ref[4:5, :]) + mod_ref[3:4, :]
    h_hi = h.astype(BF16)
    h_ref[...] = h_hi.reshape(h_ref.shape)
    h_lo = (h - h_hi.astype(F32)).astype(BF16)
    logits = (jnp.dot(h_hi, wrh_ref[...], preferred_element_type=F32)
              + jnp.dot(h_lo, wrh_ref[...], preferred_element_type=F32)
              + jnp.dot(h_hi, wrl_ref[...], preferred_element_type=F32))
    lg = logits.T[0:n_experts, :]
    mx = jnp.max(lg, axis=0, keepdims=True)
    ex = jnp.exp(lg - mx)
    probs = ex / jnp.sum(ex, axis=0, keepdims=True)
    sel = probs + rb_ref[...]
    sel_rows = [sel[e:e + 1, :] for e in range(n_experts)]
    prob_rows = [probs[e:e + 1, :] for e in range(n_experts)]
    assert per_group == 4
    group_scores = [_top2_of4(*sel_rows[per_group * g:per_group * (g + 1)]) for g in range(N_GROUPS)]
    best_group = _first_argmax(group_scores)
    v = [_select([sel_rows[per_group * g + j] for g in range(N_GROUPS)], best_group) for j in range(per_group)]
    p = [_select([prob_rows[per_group * g + j] for g in range(N_GROUPS)], best_group) for j in range(per_group)]
    i1 = _first_argmax(v)
    i2 = _first_argmax([jnp.where(i1 == j, -jnp.inf, v[j]) for j in range(per_group)])
    p1, p2 = _select(p, i1), _select(p, i2)
    e1 = best_group * per_group + i1
    e2 = best_group * per_group + i2
    eid = lax.broadcasted_iota(I32, (n_experts, tm), 0)
    onehot = jnp.where((eid == e1) | (eid == e2), 1.0, 0.0)
    prefix = jnp.dot(onehot.astype(BF16), tri_ref[...], preferred_element_type=F32)
    tot = carry_ref[:, 0:1] + prefix
    rank1 = jnp.sum(jnp.where(eid == e1, tot, 0.0), axis=0, keepdims=True)
    rank2 = jnp.sum(jnp.where(eid == e2, tot, 0.0), axis=0, keepdims=True)
    new_carry = carry_ref[...] + jnp.sum(onehot, axis=1, keepdims=True)
    carry_ref[...] = new_carry
    cnt_ref[...] = new_carry.astype(I32)
    ri_ref[0:1, :] = e1
    ri_ref[1:2, :] = e2
    ri_ref[2:3, :] = rank1.astype(I32)
    ri_ref[3:4, :] = rank2.astype(I32)
    ri_ref[4:8, :] = jnp.zeros((4, tm), I32)
    rowid = lax.broadcasted_iota(I32, (LANES, tm), 0)
    wsum = p1 + p2
    slab = jnp.where(rowid == 0, p1 / wsum, jnp.where(rowid == 1, p2 / wsum, 0.0))
    rw_ref[...] = slab.T


def _ln_router(x, y, mod, layer, lng, lnb, w_router, router_bias, seq, alpha, tm):
    t, d = x.shape
    n_experts = w_router.shape[1]
    tiles_per_seq = seq // tm
    wr = jnp.zeros((d, LANES), F32).at[:, :n_experts].set(w_router)
    wr_hi = wr.astype(BF16)
    wr_lo = (wr - wr_hi.astype(F32)).astype(BF16)
    rb = jnp.broadcast_to(router_bias.astype(F32)[:, None], (n_experts, tm))
    tri = (lax.broadcasted_iota(I32, (tm, tm), 0) < lax.broadcasted_iota(I32, (tm, tm), 1)).astype(BF16)
    tok = lambda i: (i, 0)
    const = lambda i: (0, 0)
    return pl.pallas_call(
        functools.partial(_ln_router_kernel, alpha=alpha, n_experts=n_experts),
        grid=(t // tm,),
        in_specs=[
            pl.BlockSpec((tm, d), tok),
            pl.BlockSpec((tm, d), tok),
            pl.BlockSpec((None, None, N_MOD, d), lambda i: (layer, i // tiles_per_seq, 0, 0)),
            pl.BlockSpec((1, d), const),
            pl.BlockSpec((1, d), const),
            pl.BlockSpec((d, LANES), const),
            pl.BlockSpec((d, LANES), const),
            pl.BlockSpec((n_experts, tm), const),
            pl.BlockSpec((tm, tm), const),
        ],
        out_specs=[
            pl.BlockSpec((tm, d), tok),
            pl.BlockSpec((tm, d // LANES, LANES), lambda i: (i, 0, 0)),
            pl.BlockSpec((SUBLANES, tm), lambda i: (0, i)),
            pl.BlockSpec((tm, LANES), tok),
            pl.BlockSpec((n_experts, LANES), const),
        ],
        out_shape=[
            jax.ShapeDtypeStruct((t, d), F32),
            jax.ShapeDtypeStruct((t, d // LANES, LANES), BF16),
            jax.ShapeDtypeStruct((SUBLANES, t), I32),
            jax.ShapeDtypeStruct((t, LANES), F32),
            jax.ShapeDtypeStruct((n_experts, LANES), I32),
        ],
        scratch_shapes=[pltpu.VMEM((n_experts, LANES), F32)],
        compiler_params=_params(1),
        name="ln_router",
    )(x, y, mod, lng.reshape(1, d), lnb.reshape(1, d), wr_hi, wr_lo, rb, tri)


def _row_copy(src_ref, src_row, dst_ref, dst_row, sem):
    return pltpu.make_async_copy(src_ref.at[pl.ds(src_row, 1)], dst_ref.at[pl.ds(dst_row, 1)], sem)


def _dispatch_kernel(pos1_ref, pos2_ref, h_ref, xs_in_ref, xs_ref, sem, *, tm):
    del xs_in_ref
    base = pl.program_id(0) * tm

    def start(t, carry):
        _row_copy(h_ref, t, xs_ref, pos1_ref[base + t], sem).start()
        _row_copy(h_ref, t, xs_ref, pos2_ref[base + t], sem).start()
        return carry

    lax.fori_loop(0, tm, start, 0, unroll=8)
    for _ in range(2):
        pltpu.make_async_copy(h_ref, xs_ref.at[pl.ds(0, tm)], sem).wait()


def _dispatch(h, pos1, pos2, xs0, tm):
    t = h.shape[0]
    return pl.pallas_call(
        functools.partial(_dispatch_kernel, tm=tm),
        grid_spec=pltpu.PrefetchScalarGridSpec(
            num_scalar_prefetch=2,
            grid=(t // tm,),
            in_specs=[pl.BlockSpec((tm,) + h.shape[1:], lambda i, p1, p2: (i, 0, 0)),
                      pl.BlockSpec(memory_space=pl.ANY)],
            out_specs=pl.BlockSpec(memory_space=pl.ANY),
            scratch_shapes=[pltpu.SemaphoreType.DMA(())],
        ),
        out_shape=jax.ShapeDtypeStruct(xs0.shape, h.dtype),
        input_output_aliases={3: 0},
        compiler_params=_params(1),
        name="moe_dispatch",
    )(pos1, pos2, h, xs0)


def _first_tile_of_expert(te_ref, j):
    return (j == 0) | (te_ref[j] != te_ref[jnp.maximum(j - 1, 0)])


def _expert_weight_copies(w_refs, layer, expert, col0, stage_ref, sems):
    width = stage_ref.shape[2]
    return [pltpu.make_async_copy(w.at[layer, expert].at[:, pl.ds(col0, width)], stage_ref.at[i], sems.at[i])
            for i, w in enumerate(w_refs)]


def _stream_expert_weights(te_ref, nxt_ref, w_refs, layer, stage_ref, wb_ref, sems):
    p, j = pl.program_id(0), pl.program_id(1)
    width = stage_ref.shape[2]

    def copies(expert, col_pass):
        return _expert_weight_copies(w_refs, layer, expert, pl.multiple_of(col_pass * width, width),
                                     stage_ref, sems)

    @pl.when(_first_tile_of_expert(te_ref, j))
    def _switch_expert():
        @pl.when((p == 0) & (j == 0))
        def _very_first():
            for c in copies(te_ref[0], 0):
                c.start()

        for c in copies(te_ref[j], p):
            c.wait()
        for i in range(len(w_refs)):
            wb_ref[i] = stage_ref[i].astype(BF16)
        nxt = nxt_ref[j]

        @pl.when(nxt >= 0)
        def _next_expert():
            for c in copies(nxt, p):
                c.start()

        @pl.when((nxt < 0) & (p + 1 < pl.num_programs(0)))
        def _next_pass():
            for c in copies(te_ref[0], p + 1):
                c.start()


def _moe_up_kernel(te_ref, xt_ref, rows_ref, nxt_ref, x_ref, wg_ref, wu_ref, o_ref, stage_ref, wb_ref, sems,
                   *, layer):
    del xt_ref
    j = pl.program_id(1)
    half = x_ref.shape[0] // 2
    _stream_expert_weights(te_ref, nxt_ref, (wg_ref, wu_ref), layer, stage_ref, wb_ref, sems)

    def half_tile(r0, used):
        @pl.when(used)
        def _compute():
            x = x_ref[r0:r0 + half].reshape(half, wb_ref.shape[1])
            gate = jnp.dot(x, wb_ref[0], preferred_element_type=F32)
            up = jnp.dot(x, wb_ref[1], preferred_element_type=F32)
            o_ref[r0:r0 + half, :] = (_silu(gate) * up).astype(o_ref.dtype)

        @pl.when(jnp.logical_not(used))
        def _unused():
            o_ref[r0:r0 + half, :] = jnp.zeros((half, o_ref.shape[1]), o_ref.dtype)

    half_tile(0, rows_ref[j] > 0)
    half_tile(half, rows_ref[j] > half)


def _moe_up(xs, w_gate, w_up, layer, te, xt, rows, nxt, tm, bf):
    n_rows = xs.shape[0]
    d, f = w_gate.shape[2], w_gate.shape[3]
    n_tiles = n_rows // tm
    return pl.pallas_call(
        functools.partial(_moe_up_kernel, layer=layer),
        grid_spec=pltpu.PrefetchScalarGridSpec(
            num_scalar_prefetch=4,
            grid=(f // bf, n_tiles),
            in_specs=[pl.BlockSpec((tm, d // LANES, LANES), lambda jf, j, te, xt, rows, nxt: (xt[j], 0, 0)),
                      pl.BlockSpec(memory_space=pl.ANY), pl.BlockSpec(memory_space=pl.ANY)],
            out_specs=pl.BlockSpec((tm, bf), lambda jf, j, te, xt, rows, nxt: (j, jf)),
            scratch_shapes=[pltpu.VMEM((2, d, bf), F32), pltpu.VMEM((2, d, bf), BF16),
                            pltpu.SemaphoreType.DMA((2,))],
        ),
        out_shape=jax.ShapeDtypeStruct((n_rows, f), BF16),
        compiler_params=_params(2),
        name="moe_up",
    )(te, xt, rows, nxt, xs, w_gate, w_up)


def _moe_down_kernel(te_ref, xt_ref, rows_ref, nxt_ref, x_ref, w_ref, o_ref, stage_ref, wb_ref, sems, *, layer):
    del xt_ref
    j = pl.program_id(1)
    half = x_ref.shape[0] // 2
    _stream_expert_weights(te_ref, nxt_ref, (w_ref,), layer, stage_ref, wb_ref, sems)

    def half_tile(r0, used):
        @pl.when(used)
        def _compute():
            acc = jnp.dot(x_ref[r0:r0 + half, :], wb_ref[0], preferred_element_type=F32)
            o_ref[r0:r0 + half] = acc.astype(o_ref.dtype).reshape((half,) + o_ref.shape[1:])

        @pl.when(jnp.logical_not(used))
        def _unused():
            o_ref[r0:r0 + half] = jnp.zeros((half,) + o_ref.shape[1:], o_ref.dtype)

    half_tile(0, rows_ref[j] > 0)
    half_tile(half, rows_ref[j] > half)


def _moe_down(hid, w_down, layer, te, xt, rows, nxt, tm, bn):
    n_rows, f = hid.shape
    d = w_down.shape[3]
    n_tiles = n_rows // tm
    return pl.pallas_call(
        functools.partial(_moe_down_kernel, layer=layer),
        grid_spec=pltpu.PrefetchScalarGridSpec(
            num_scalar_prefetch=4,
            grid=(d // bn, n_tiles),
            in_specs=[
                pl.BlockSpec((tm, f), lambda jn, j, te, xt, rows, nxt: (xt[j], 0)),
                pl.BlockSpec(memory_space=pl.ANY),
            ],
            out_specs=pl.BlockSpec((tm, bn // LANES, LANES), lambda jn, j, te, xt, rows, nxt: (j, jn, 0)),
            scratch_shapes=[pltpu.VMEM((1, f, bn), F32), pltpu.VMEM((1, f, bn), BF16),
                            pltpu.SemaphoreType.DMA((1,))],
        ),
        out_shape=jax.ShapeDtypeStruct((n_rows, d // LANES, LANES), BF16),
        compiler_params=_params(2),
        name="moe_down",
    )(te, xt, rows, nxt, hid, w_down)


def _combine_ln_kernel(pos1_ref, pos2_ref, x_ref, rw_ref, mod_ref, lng_ref, lnb_ref, nmod_ref, ys_ref,
                       *rest, alpha, emit_h):
    if emit_h:
        xo_ref, h_ref, buf_ref, sem = rest
    else:
        xo_ref, buf_ref, sem = rest
    tm = x_ref.shape[0]
    i = pl.program_id(0)

    def gather(tile, slot):
        base = tile * tm

        def start(t, carry):
            _row_copy(ys_ref, pos1_ref[base + t], buf_ref.at[slot, 0], t, sem.at[slot]).start()
            _row_copy(ys_ref, pos2_ref[base + t], buf_ref.at[slot, 1], t, sem.at[slot]).start()
            return carry

        lax.fori_loop(0, tm, start, 0, unroll=8)

    @pl.when(i == 0)
    def _first_tile():
        gather(0, 0)

    @pl.when(i + 1 < pl.num_programs(0))
    def _prefetch():
        gather(i + 1, (i + 1) % 2)

    slot = i % 2
    for k in range(2):
        pltpu.make_async_copy(ys_ref.at[pl.ds(0, tm)], buf_ref.at[slot, k], sem.at[slot]).wait()
    rw = rw_ref[...]
    y1 = buf_ref[slot, 0].reshape(x_ref.shape).astype(F32)
    y2 = buf_ref[slot, 1].reshape(x_ref.shape).astype(F32)
    y = rw[:, 0:1] * y1 + rw[:, 1:2] * y2
    xn = _residual_ln(x_ref[...], y, mod_ref, lng_ref, lnb_ref, alpha, 5)
    xo_ref[...] = xn
    if emit_h:
        h_ref[...] = (xn * (1.0 + nmod_ref[1:2, :]) + nmod_ref[0:1, :]).astype(BF16)


def _combine_ln(x, ys, pos1, pos2, rw, mod, layer, next_layer, lng, lnb, seq, alpha, tm):
    t, d = x.shape
    tiles_per_seq = seq // tm
    emit_h = next_layer is not None
    tok = lambda i, p1, p2: (i, 0)
    const = lambda i, p1, p2: (0, 0)
    mod_spec = lambda l: pl.BlockSpec((None, None, N_MOD, d), lambda i, p1, p2: (l, i // tiles_per_seq, 0, 0))
    out_specs = [pl.BlockSpec((tm, d), tok)]
    out_shape = [jax.ShapeDtypeStruct((t, d), F32)]
    if emit_h:
        out_specs.append(pl.BlockSpec((tm, d), tok))
        out_shape.append(jax.ShapeDtypeStruct((t, d), BF16))
    outs = pl.pallas_call(
        functools.partial(_combine_ln_kernel, alpha=alpha, emit_h=emit_h),
        grid_spec=pltpu.PrefetchScalarGridSpec(
            num_scalar_prefetch=2,
            grid=(t // tm,),
            in_specs=[
                pl.BlockSpec((tm, d), tok),
                pl.BlockSpec((tm, LANES), tok),
                mod_spec(layer),
                pl.BlockSpec((1, d), const),
                pl.BlockSpec((1, d), const),
                mod_spec(next_layer if emit_h else layer),
                pl.BlockSpec(memory_space=pl.ANY),
            ],
            out_specs=out_specs,
            scratch_shapes=[pltpu.VMEM((2, 2, tm, d // LANES, LANES), BF16), pltpu.SemaphoreType.DMA((2,))],
        ),
        out_shape=out_shape,
        compiler_params=_params(1),
        name="moe_combine_ln",
    )(pos1, pos2, x, rw, mod, lng.reshape(1, d), lnb.reshape(1, d), mod, ys)
    return (outs[0], outs[1]) if emit_h else (outs[0], None)


def _tiles(t, seq, d, d_ff):
    return dict(
        bm=min(1024, seq),
        bn1=min(1024, d // 2),
        bn2=min(256, d // 2),
        bn_merge=min(512, d // 2),
        mod_bn=min(1024, d),
        conv_tm=min(256, seq),
        conv_b_tm=min(512, seq),
        ln_tm=min(256, seq),
        moe_tm=min(512, seq),
        moe_bf=min(512, d_ff),
        moe_bn=min(4096, d),
        dispatch_tm=min(512, seq),
    )


def _moe_schedule(counts, ranks, experts, tm, n_tiles):
    n_experts = counts.shape[0]
    tiles_per_expert = (counts + tm - 1) // tm
    tile_end = jnp.cumsum(tiles_per_expert)
    tile_start = tile_end - tiles_per_expert
    total = tile_end[-1]
    row_start = tile_start * tm
    pos = [row_start[experts[k]] + ranks[k] for k in range(2)]
    j = jnp.arange(n_tiles, dtype=I32)
    xt = jnp.minimum(j, total - 1)
    te = jnp.minimum(jnp.sum((xt[:, None] >= tile_end[None, :]).astype(I32), axis=1), n_experts - 1)
    rows = jnp.where(j < total, jnp.clip(counts[te] - (j - tile_start[te]) * tm, 0, tm), 0)
    eid = jnp.arange(n_experts, dtype=I32)
    later = jnp.where((eid[None, :] > eid[:, None]) & (tiles_per_expert[None, :] > 0), eid[None, :], n_experts)
    next_expert = jnp.min(later, axis=1)
    nxt = jnp.where(next_expert < n_experts, next_expert, -1)[te]
    return (pos[0].astype(I32), pos[1].astype(I32), te.astype(I32), xt.astype(I32), rows.astype(I32),
            nxt.astype(I32))


def kernel(x, c, w_ada, b_ada, w_in, conv_a_w, conv_a_b, ln_a_g, ln_a_b, conv_b_w, w_pa, w_pb, w_o,
           ln_mix_g, ln_mix_b, w_router, router_bias, w_gate, w_up, w_down, ln_ffn_g, ln_ffn_b):
    batch, seq, d = x.shape
    n_layers = w_in.shape[0]
    d_conf = conv_a_w.shape[2]
    d_short = conv_b_w.shape[2]
    n_experts, d_ff = w_gate.shape[1], w_gate.shape[3]
    t = batch * seq
    alpha = float((2 * n_layers) ** 0.25)
    ts = _tiles(t, seq, d, d_ff)
    bm, bn1, bn2 = ts["bm"], ts["bn1"], ts["bn2"]
    c_aval, c_agate = 0, d_conf
    c_sb, c_sc, c_sx = 2 * d_conf, 2 * d_conf + d_short, 2 * d_conf + 2 * d_short
    c_ga = 2 * d_conf + 3 * d_short
    n_moe_tiles = (2 * t) // ts["moe_tm"] + n_experts
    n_moe_rows = n_moe_tiles * ts["moe_tm"]

    mod = _modulation(c, w_ada, b_ada, ts["mod_bn"])
    xc = x.reshape(t, d)
    h = _modulate(xc, mod, 0, seq, ts["ln_tm"])
    xs = jnp.zeros((n_moe_rows, d // LANES, LANES), BF16)
    for l in range(n_layers):
        w_in_l = lambda col0: (w_in, l, col0, 0)
        ya_pre = _matmul([h], [w_in_l(c_aval), w_in_l(c_agate)], [], _ep_glu, d_conf, F32, bm, bn2, "inproj_glu")
        cx = _matmul([h], [w_in_l(c_sc), w_in_l(c_sx)], [], _ep_mul, d_short, F32, bm, bn2, "inproj_cx")
        sb = _matmul([h], [w_in_l(c_sb)], [], _ep_id, d_short, BF16, bm, bn1, "inproj_sb")
        gates = _matmul([h], [w_in_l(c_ga)], [], _ep_sigmoid, 2 * d, BF16, bm, bn1, "inproj_gates")
        ya = _causal_conv(ya_pre, conv_a_w[l], seq, ts["conv_tm"], "ln_silu",
                          (conv_a_b[l], ln_a_g[l], ln_a_b[l]), "conv_a")
        yb = _causal_conv(cx, conv_b_w[l], seq, ts["conv_b_tm"], "gate", (sb,), "conv_b")
        m = _matmul([ya, yb], [(w_pa, l, 0, 0), (w_pb, l, 0, 1)], [(gates, 0), (gates, d)],
                    _ep_merge, d, BF16, bm, ts["bn_merge"], "proj_merge")
        y = _matmul([m], [(w_o, l, 0, 0)], [], _ep_id, d, BF16, bm, bn1, "out_proj")
        xc, h2, ri, rw, cnt = _ln_router(xc, y, mod, l, ln_mix_g[l], ln_mix_b[l], w_router, router_bias,
                                         seq, alpha, ts["ln_tm"])
        pos1, pos2, te, xt, rows, nxt = _moe_schedule(cnt[:, 0], (ri[2], ri[3]), (ri[0], ri[1]),
                                                      ts["moe_tm"], n_moe_tiles)
        xs = _dispatch(h2, pos1, pos2, xs, ts["dispatch_tm"])
        hid = _moe_up(xs, w_gate, w_up, l, te, xt, rows, nxt, ts["moe_tm"], ts["moe_bf"])
        ys = _moe_down(hid, w_down, l, te, xt, rows, nxt, ts["moe_tm"], ts["moe_bn"])
        xc, h = _combine_ln(xc, ys, pos1, pos2, rw, mod, l, l + 1 if l + 1 < n_layers else None,
                            ln_ffn_g[l], ln_ffn_b[l], seq, alpha, ts["ln_tm"])
    return xc.reshape(batch, seq, d)
```

```python
import functools

import jax
import jax.numpy as jnp
from jax import lax
from jax.experimental import pallas as pl
from jax.experimental.pallas import tpu as pltpu

F32 = jnp.float32
BF16 = jnp.bfloat16
I32 = jnp.int32

LN_EPS = 1e-5
N_GROUPS = 4
N_MOD = 6
V7X_VMEM_LIMIT_BYTES = 60000 * 1024
LANES = 128
SUBLANES = 8
HALO_ROWS = 32
MM_ROW_BLOCK = 256
CAST_ROWS = 64
CAST_LANES = 512
CONV_ROW_CHUNK = 128


def _params(n_grid):
    return pltpu.CompilerParams(
        dimension_semantics=("arbitrary",) * n_grid, vmem_limit_bytes=V7X_VMEM_LIMIT_BYTES)


def _sigmoid(v):
    return 1.0 / (1.0 + jnp.exp(-v))


def _silu(v):
    return v * _sigmoid(v)


def _layer_norm(z, g, b):
    mu = jnp.mean(z, axis=-1, keepdims=True)
    zc = z - mu
    var = jnp.mean(zc * zc, axis=-1, keepdims=True)
    return zc * lax.rsqrt(var + LN_EPS) * g + b


def _mod_kernel(c_ref, w_ref, b_ref, o_ref):
    cond = _silu(c_ref[...])
    acc = jnp.dot(cond.astype(BF16), w_ref[...].astype(BF16), preferred_element_type=F32)
    o_ref[...] = acc + b_ref[...]


def _modulation(c, w_ada, b_ada, bn):
    n_layers, d, n = w_ada.shape
    b = c.shape[0]
    c_pad = jnp.zeros((SUBLANES, d), F32).at[:b].set(c)
    out = pl.pallas_call(
        _mod_kernel,
        grid=(n_layers, n // bn),
        in_specs=[
            pl.BlockSpec((SUBLANES, d), lambda l, j: (0, 0)),
            pl.BlockSpec((None, d, bn), lambda l, j: (l, 0, j)),
            pl.BlockSpec((None, 1, bn), lambda l, j: (l, 0, j)),
        ],
        out_specs=pl.BlockSpec((None, SUBLANES, bn), lambda l, j: (l, 0, j)),
        out_shape=jax.ShapeDtypeStruct((n_layers, SUBLANES, n), F32),
        compiler_params=_params(2),
        name="adaln_mod",
    )(c_pad, w_ada, b_ada.reshape(n_layers, 1, n))
    return out[:, :b].reshape(n_layers, b, N_MOD, d)


def _modulate_kernel(x_ref, mod_ref, h_ref, *, shift_row, scale_row):
    sc = mod_ref[scale_row:scale_row + 1, :]
    sh = mod_ref[shift_row:shift_row + 1, :]
    h_ref[...] = (x_ref[...] * (1.0 + sc) + sh).astype(BF16)


def _modulate(x2, mod, layer, seq, tm):
    t, d = x2.shape
    tiles_per_seq = seq // tm
    return pl.pallas_call(
        functools.partial(_modulate_kernel, shift_row=0, scale_row=1),
        grid=(t // tm,),
        in_specs=[
            pl.BlockSpec((tm, d), lambda i: (i, 0)),
            pl.BlockSpec((None, None, N_MOD, d), lambda i: (layer, i // tiles_per_seq, 0, 0)),
        ],
        out_specs=pl.BlockSpec((tm, d), lambda i: (i, 0)),
        out_shape=jax.ShapeDtypeStruct((t, d), BF16),
        compiler_params=_params(1),
        name="modulate_in",
    )(x2, mod)


def _mm_kernel(*refs, n_x, w_cols, w_x, n_e, epilogue):
    n_w = len(w_x)
    x_refs = refs[:n_x]
    w_refs = refs[n_x:n_x + n_w]
    e_refs = refs[n_x + n_w:n_x + n_w + n_e]
    o_ref, stage_ref, wb_ref, sems = refs[n_x + n_w + n_e:]
    j, i = pl.program_id(0), pl.program_id(1)
    bn = wb_ref.shape[2]

    def copies(col_tile):
        return [pltpu.make_async_copy(
            w.at[lead].at[:, pl.ds(pl.multiple_of((cb + col_tile) * bn, bn), bn)], stage_ref.at[k], sems.at[k])
            for k, (w, (lead, cb)) in enumerate(zip(w_refs, w_cols))]

    @pl.when(i == 0)
    def _next_column_tile():
        @pl.when(j == 0)
        def _very_first():
            for c in copies(0):
                c.start()

        for c in copies(j):
            c.wait()
        for k in range(n_w):
            wb_ref[k] = stage_ref[k].astype(BF16)

        @pl.when(j + 1 < pl.num_programs(0))
        def _prefetch():
            for c in copies(j + 1):
                c.start()

    rows = min(MM_ROW_BLOCK, o_ref.shape[0])
    for r0 in range(0, o_ref.shape[0], rows):
        accs = [jnp.dot(x_refs[w_x[k]][r0:r0 + rows, :], wb_ref[k], preferred_element_type=F32)
                for k in range(n_w)]
        extras = [e[r0:r0 + rows, :] for e in e_refs]
        o_ref[r0:r0 + rows, :] = epilogue(accs, extras).astype(o_ref.dtype)


def _matmul(xs, ws, extras, epilogue, n_out, out_dtype, bm, bn, name):
    m, k = xs[0].shape
    n_w = len(ws)
    in_specs = [pl.BlockSpec((bm, x.shape[1]), lambda j, i: (i, 0)) for x in xs]
    args = list(xs)
    for w, lead, col0, _ in ws:
        assert col0 % bn == 0 and w.shape[1] == k
        in_specs.append(pl.BlockSpec(memory_space=pl.ANY))
        args.append(w)
    for e, col0 in extras:
        assert col0 % bn == 0
        in_specs.append(pl.BlockSpec((bm, bn), functools.partial(
            lambda j, i, cb: (i, cb + j), cb=col0 // bn)))
        args.append(e)
    return pl.pallas_call(
        functools.partial(_mm_kernel, n_x=len(xs), w_cols=tuple((w[1], w[2] // bn) for w in ws),
                          w_x=tuple(w[3] for w in ws), n_e=len(extras), epilogue=epilogue),
        grid=(n_out // bn, m // bm),
        in_specs=in_specs,
        out_specs=pl.BlockSpec((bm, bn), lambda j, i: (i, j)),
        out_shape=jax.ShapeDtypeStruct((m, n_out), out_dtype),
        scratch_shapes=[pltpu.VMEM((n_w, k, bn), F32), pltpu.VMEM((n_w, k, bn), BF16),
                        pltpu.SemaphoreType.DMA((n_w,))],
        compiler_params=_params(2),
        name=name,
    )(*args)


def _ep_glu(accs, _):
    return accs[0] * _sigmoid(accs[1])


def _ep_mul(accs, _):
    return accs[0] * accs[1]


def _ep_id(accs, _):
    return accs[0]


def _ep_sigmoid(accs, _):
    return _sigmoid(accs[0])


def _ep_merge(accs, extras):
    return extras[0].astype(F32) * accs[0] + extras[1].astype(F32) * accs[1]


def _conv_shifts(taps):
    starts = [HALO_ROWS - (taps - 1) + k for k in range(taps)]
    return starts, sorted({s % SUBLANES for s in starts} - {0})


def _conv_kernel(*refs, taps, tm, tiles_per_seq, mode):
    if mode == "ln_silu":
        cur_ref, halo_ref, w_ref, b_ref, g_ref, beta_ref, o_ref, buf_ref, sh_ref, y_ref = refs
    else:
        cur_ref, halo_ref, w_ref, gate_ref, o_ref, buf_ref, sh_ref = refs
    c = cur_ref.shape[1]
    first = (pl.program_id(0) % tiles_per_seq) == 0
    buf_ref[0:HALO_ROWS, :] = jnp.where(first, 0.0, halo_ref[...])
    buf_ref[HALO_ROWS:HALO_ROWS + tm, :] = cur_ref[...]
    starts, residues = _conv_shifts(taps)
    sh_rows = sh_ref.shape[1]
    for n, r in enumerate(residues):
        sh_ref[n] = buf_ref[r:r + sh_rows, :]
    row_chunk = min(CONV_ROW_CHUNK, tm)
    n_vregs = row_chunk // SUBLANES
    n_lane_chunks = c // LANES

    def chunk(idx, carry):
        r0 = pl.multiple_of((idx // n_lane_chunks) * row_chunk, row_chunk)
        lanes = pl.ds(pl.multiple_of((idx % n_lane_chunks) * LANES, LANES), LANES)
        accs = [jnp.zeros((SUBLANES, LANES), F32)] * n_vregs
        for kk in range(taps):
            q, r = divmod(starts[kk], SUBLANES)
            wk = jnp.broadcast_to(w_ref[kk:kk + 1, lanes], (SUBLANES, LANES))
            src = buf_ref if r == 0 else sh_ref.at[residues.index(r)]
            for v in range(n_vregs):
                rows = pl.ds(pl.multiple_of(r0 + (q + v) * SUBLANES, SUBLANES), SUBLANES)
                accs[v] = accs[v] + wk * src[rows, lanes]
        if mode == "ln_silu":
            bias = jnp.broadcast_to(b_ref[:, lanes], (SUBLANES, LANES))
            for v in range(n_vregs):
                y_ref[pl.ds(pl.multiple_of(r0 + v * SUBLANES, SUBLANES), SUBLANES), lanes] = accs[v] + bias
        else:
            for v in range(0, n_vregs, 2):
                rows = pl.ds(pl.multiple_of(r0 + v * SUBLANES, 2 * SUBLANES), 2 * SUBLANES)
                acc = jnp.concatenate([accs[v], accs[v + 1]], axis=0)
                o_ref[rows, lanes] = (gate_ref[rows, lanes].astype(F32) * acc).astype(o_ref.dtype)
        return carry

    lax.fori_loop(0, (tm // row_chunk) * n_lane_chunks, chunk, 0)
    if mode == "ln_silu":
        o_ref[...] = _silu(_layer_norm(y_ref[...], g_ref[...], beta_ref[...])).astype(o_ref.dtype)


def _causal_conv(x, w, seq, tm, mode, extra, name):
    t, c = x.shape
    taps = w.shape[0]
    assert taps - 1 <= HALO_ROWS and tm % HALO_ROWS == 0
    tiles_per_seq = seq // tm
    halo_blocks = tm // HALO_ROWS
    row = lambda i: (0, 0)
    in_specs = [
        pl.BlockSpec((tm, c), lambda i: (i, 0)),
        pl.BlockSpec((HALO_ROWS, c), lambda i: (jnp.maximum(i * halo_blocks - 1, 0), 0)),
        pl.BlockSpec((taps, c), row),
    ]
    _, residues = _conv_shifts(taps)
    scratch = [pltpu.VMEM((HALO_ROWS + tm, c), F32),
               pltpu.VMEM((max(len(residues), 1), HALO_ROWS + tm - SUBLANES, c), F32)]
    if mode == "ln_silu":
        in_specs += [pl.BlockSpec((1, c), row)] * 3
        args = (x, x, w) + tuple(v.reshape(1, c) for v in extra)
        scratch.append(pltpu.VMEM((tm, c), F32))
    else:
        in_specs.append(pl.BlockSpec((tm, c), lambda i: (i, 0)))
        args = (x, x, w) + tuple(extra)
    return pl.pallas_call(
        functools.partial(_conv_kernel, taps=taps, tm=tm, tiles_per_seq=tiles_per_seq, mode=mode),
        grid=(t // tm,),
        in_specs=in_specs,
        out_specs=pl.BlockSpec((tm, c), lambda i: (i, 0)),
        out_shape=jax.ShapeDtypeStruct((t, c), BF16),
        scratch_shapes=scratch,
        compiler_params=_params(1),
        name=name,
    )(*args)


def _residual_ln(x, y, mod_ref, lng_ref, lnb_ref, alpha, gate_row):
    z = alpha * x + mod_ref[gate_row:gate_row + 1, :] * y
    return _layer_norm(z, lng_ref[...], lnb_ref[...])


def _top2_of4(a, b, c, d):
    hi1, lo1 = jnp.maximum(a, b), jnp.minimum(a, b)
    hi2, lo2 = jnp.maximum(c, d), jnp.minimum(c, d)
    return jnp.maximum(hi1, hi2) + jnp.maximum(jnp.minimum(hi1, hi2), jnp.maximum(lo1, lo2))


def _first_argmax(vals):
    idx = jnp.zeros(vals[0].shape, I32)
    best = vals[0]
    for j in range(1, len(vals)):
        take = vals[j] > best
        idx = jnp.where(take, j, idx)
        best = jnp.where(take, vals[j], best)
    return idx


def _select(rows, idx):
    out = rows[0]
    for j in range(1, len(rows)):
        out = jnp.where(idx == j, rows[j], out)
    return out


def _ln_router_kernel(x_ref, y_ref, mod_ref, lng_ref, lnb_ref, wrh_ref, wrl_ref, rb_ref, tri_ref,
                      xo_ref, h_ref, ri_ref, rw_ref, cnt_ref, carry_ref, *, alpha, n_experts):
    tm = x_ref.shape[0]
    per_group = n_experts // N_GROUPS

    @pl.when(pl.program_id(0) == 0)
    def _init():
        carry_ref[...] = jnp.zeros_like(carry_ref)

    xn = _residual_ln(x_ref[...], y_ref[...].astype(F32), mod_ref, lng_ref, lnb_ref, alpha, 2)
    xo_ref[...] = xn
    h = xn * (1.0 + mod_ref[4:5, :]) + mod_ref[3:4, :]
    h_hi = h.astype(BF16)
    h_ref[...] = h_hi.reshape(h_ref.shape)
    h_lo = (h - h_hi.astype(F32)).astype(BF16)
    logits = (jnp.dot(h_hi, wrh_ref[...], preferred_element_type=F32)
              + jnp.dot(h_lo, wrh_ref[...], preferred_element_type=F32)
              + jnp.dot(h_hi, wrl_ref[...], preferred_element_type=F32))
    lg = logits.T[0:n_experts, :]
    mx = jnp.max(lg, axis=0, keepdims=True)
    ex = jnp.exp(lg - mx)
    probs = ex / jnp.sum(ex, axis=0, keepdims=True)
    sel = probs + rb_ref[...]
    sel_rows = [sel[e:e + 1, :] for e in range(n_experts)]
    prob_rows = [probs[e:e + 1, :] for e in range(n_experts)]
    assert per_group == 4
    group_scores = [_top2_of4(*sel_rows[per_group * g:per_group * (g + 1)]) for g in range(N_GROUPS)]
    best_group = _first_argmax(group_scores)
    v = [_select([sel_rows[per_group * g + j] for g in range(N_GROUPS)], best_group) for j in range(per_group)]
    p = [_select([prob_rows[per_group * g + j] for g in range(N_GROUPS)], best_group) for j in range(per_group)]
    i1 = _first_argmax(v)
    i2 = _first_argmax([jnp.where(i1 == j, -jnp.inf, v[j]) for j in range(per_group)])
    p1, p2 = _select(p, i1), _select(p, i2)
    e1 = best_group * per_group + i1
    e2 = best_group * per_group + i2
    eid = lax.broadcasted_iota(I32, (n_experts, tm), 0)
    onehot = jnp.where((eid == e1) | (eid == e2), 1.0, 0.0)
    prefix = jnp.dot(onehot.astype(BF16), tri_ref[...], preferred_element_type=F32)
    tot = carry_ref[:, 0:1] + prefix
    rank1 = jnp.sum(jnp.where(eid == e1, tot, 0.0), axis=0, keepdims=True)
    rank2 = jnp.sum(jnp.where(eid == e2, tot, 0.0), axis=0, keepdims=True)
    new_carry = carry_ref[...] + jnp.sum(onehot, axis=1, keepdims=True)
    carry_ref[...] = new_carry
    cnt_ref[...] = new_carry.astype(I32)
    ri_ref[0:1, :] = e1
    ri_ref[1:2, :] = e2
    ri_ref[2:3, :] = rank1.astype(I32)
    ri_ref[3:4, :] = rank2.astype(I32)
    ri_ref[4:8, :] = jnp.zeros((4, tm), I32)
    rowid = lax.broadcasted_iota(I32, (LANES, tm), 0)
    wsum = p1 + p2
    slab = jnp.where(rowid == 0, p1 / wsum, jnp.where(rowid == 1, p2 / wsum, 0.0))
    rw_ref[...] = slab.T


def _ln_router(x, y, mod, layer, lng, lnb, w_router, router_bias, seq, alpha, tm):
    t, d = x.shape
    n_experts = w_router.shape[1]
    tiles_per_seq = seq // tm
    wr = jnp.zeros((d, LANES), F32).at[:, :n_experts].set(w_router)
    wr_hi = wr.astype(BF16)
    wr_lo = (wr - wr_hi.astype(F32)).astype(BF16)
    rb = jnp.broadcast_to(router_bias.astype(F32)[:, None], (n_experts, tm))
    tri = (lax.broadcasted_iota(I32, (tm, tm), 0) < lax.broadcasted_iota(I32, (tm, tm), 1)).astype(BF16)
    tok = lambda i: (i, 0)
    const = lambda i: (0, 0)
    return pl.pallas_call(
        functools.partial(_ln_router_kernel, alpha=alpha, n_experts=n_experts),
        grid=(t // tm,),
        in_specs=[
            pl.BlockSpec((tm, d), tok),
            pl.BlockSpec((tm, d), tok),
            pl.BlockSpec((None, None, N_MOD, d), lambda i: (layer, i // tiles_per_seq, 0, 0)),
            pl.BlockSpec((1, d), const),
            pl.BlockSpec((1, d), const),
            pl.BlockSpec((d, LANES), const),
            pl.BlockSpec((d, LANES), const),
            pl.BlockSpec((n_experts, tm), const),
            pl.BlockSpec((tm, tm), const),
        ],
        out_specs=[
            pl.BlockSpec((tm, d), tok),
            pl.BlockSpec((tm, d // LANES, LANES), lambda i: (i, 0, 0)),
            pl.BlockSpec((SUBLANES, tm), lambda i: (0, i)),
            pl.BlockSpec((tm, LANES), tok),
            pl.BlockSpec((n_experts, LANES), const),
        ],
        out_shape=[
            jax.ShapeDtypeStruct((t, d), F32),
            jax.ShapeDtypeStruct((t, d // LANES, LANES), BF16),
            jax.ShapeDtypeStruct((SUBLANES, t), I32),
            jax.ShapeDtypeStruct((t, LANES), F32),
            jax.ShapeDtypeStruct((n_experts, LANES), I32),
        ],
        scratch_shapes=[pltpu.VMEM((n_experts, LANES), F32)],
        compiler_params=_params(1),
        name="ln_router",
    )(x, y, mod, lng.reshape(1, d), lnb.reshape(1, d), wr_hi, wr_lo, rb, tri)


def _row_copy(src_ref, src_row, dst_ref, dst_row, sem):
    return pltpu.make_async_copy(src_ref.at[pl.ds(src_row, 1)], dst_ref.at[pl.ds(dst_row, 1)], sem)


def _dispatch_kernel(pos1_ref, pos2_ref, h_ref, xs_in_ref, xs_ref, sem, *, tm):
    del xs_in_ref
    base = pl.program_id(0) * tm

    def start(t, carry):
        _row_copy(h_ref, t, xs_ref, pos1_ref[base + t], sem).start()
        _row_copy(h_ref, t, xs_ref, pos2_ref[base + t], sem).start()
        return carry

    lax.fori_loop(0, tm, start, 0, unroll=8)
    for _ in range(2):
        pltpu.make_async_copy(h_ref, xs_ref.at[pl.ds(0, tm)], sem).wait()


def _dispatch(h, pos1, pos2, xs0, tm):
    t = h.shape[0]
    return pl.pallas_call(
        functools.partial(_dispatch_kernel, tm=tm),
        grid_spec=pltpu.PrefetchScalarGridSpec(
            num_scalar_prefetch=2,
            grid=(t // tm,),
            in_specs=[pl.BlockSpec((tm,) + h.shape[1:], lambda i, p1, p2: (i, 0, 0)),
                      pl.BlockSpec(memory_space=pl.ANY)],
            out_specs=pl.BlockSpec(memory_space=pl.ANY),
            scratch_shapes=[pltpu.SemaphoreType.DMA(())],
        ),
        out_shape=jax.ShapeDtypeStruct(xs0.shape, h.dtype),
        input_output_aliases={3: 0},
        compiler_params=_params(1),
        name="moe_dispatch",
    )(pos1, pos2, h, xs0)


def _first_tile_of_expert(te_ref, j):
    return (j == 0) | (te_ref[j] != te_ref[jnp.maximum(j - 1, 0)])


def _expert_weight_copies(w_refs, layer, expert, col0, stage_ref, sems):
    width = stage_ref.shape[2]
    return [pltpu.make_async_copy(w.at[layer, expert].at[:, pl.ds(col0, width)], stage_ref.at[i], sems.at[i])
            for i, w in enumerate(w_refs)]


def _cast_blocks(stage_ref, wb_ref):
    n_w, k, n = stage_ref.shape
    rows, lanes = min(CAST_ROWS, k), min(CAST_LANES, n)
    n_lane_blocks = n // lanes

    def block(idx, carry):
        r = pl.ds(pl.multiple_of((idx // n_lane_blocks) * rows, rows), rows)
        c = pl.ds(pl.multiple_of((idx % n_lane_blocks) * lanes, lanes), lanes)
        for i in range(n_w):
            wb_ref[i, r, c] = stage_ref[i, r, c].astype(BF16)
        return carry

    lax.fori_loop(0, (k // rows) * n_lane_blocks, block, 0)


def _stream_expert_weights(te_ref, nxt_ref, w_refs, layer, stage_ref, wb_ref, sems, cast_in_blocks):
    p, j = pl.program_id(0), pl.program_id(1)
    width = stage_ref.shape[2]

    def copies(expert, col_pass):
        return _expert_weight_copies(w_refs, layer, expert, pl.multiple_of(col_pass * width, width),
                                     stage_ref, sems)

    @pl.when(_first_tile_of_expert(te_ref, j))
    def _switch_expert():
        @pl.when((p == 0) & (j == 0))
        def _very_first():
            for c in copies(te_ref[0], 0):
                c.start()

        for c in copies(te_ref[j], p):
            c.wait()
        if cast_in_blocks:
            _cast_blocks(stage_ref, wb_ref)
        else:
            wb_ref[...] = stage_ref[...].astype(BF16)
        nxt = nxt_ref[j]

        @pl.when(nxt >= 0)
        def _next_expert():
            for c in copies(nxt, p):
                c.start()

        @pl.when((nxt < 0) & (p + 1 < pl.num_programs(0)))
        def _next_pass():
            for c in copies(te_ref[0], p + 1):
                c.start()


def _moe_up_kernel(te_ref, xt_ref, rows_ref, nxt_ref, x_ref, wg_ref, wu_ref, o_ref, stage_ref, wb_ref, sems,
                   *, layer):
    del xt_ref
    j = pl.program_id(1)
    half = x_ref.shape[0] // 2
    _stream_expert_weights(te_ref, nxt_ref, (wg_ref, wu_ref), layer, stage_ref, wb_ref, sems, True)

    def half_tile(r0, used):
        @pl.when(used)
        def _compute():
            x = x_ref[r0:r0 + half].reshape(half, wb_ref.shape[1])
            gate = jnp.dot(x, wb_ref[0], preferred_element_type=F32)
            up = jnp.dot(x, wb_ref[1], preferred_element_type=F32)
            o_ref[r0:r0 + half, :] = (_silu(gate) * up).astype(o_ref.dtype)

        @pl.when(jnp.logical_not(used))
        def _unused():
            o_ref[r0:r0 + half, :] = jnp.zeros((half, o_ref.shape[1]), o_ref.dtype)

    half_tile(0, rows_ref[j] > 0)
    half_tile(half, rows_ref[j] > half)


def _moe_up(xs, w_gate, w_up, layer, te, xt, rows, nxt, tm, bf):
    n_rows = xs.shape[0]
    d, f = w_gate.shape[2], w_gate.shape[3]
    n_tiles = n_rows // tm
    return pl.pallas_call(
        functools.partial(_moe_up_kernel, layer=layer),
        grid_spec=pltpu.PrefetchScalarGridSpec(
            num_scalar_prefetch=4,
            grid=(f // bf, n_tiles),
            in_specs=[pl.BlockSpec((tm, d // LANES, LANES), lambda jf, j, te, xt, rows, nxt: (xt[j], 0, 0)),
                      pl.BlockSpec(memory_space=pl.ANY), pl.BlockSpec(memory_space=pl.ANY)],
            out_specs=pl.BlockSpec((tm, bf), lambda jf, j, te, xt, rows, nxt: (j, jf)),
            scratch_shapes=[pltpu.VMEM((2, d, bf), F32), pltpu.VMEM((2, d, bf), BF16),
                            pltpu.SemaphoreType.DMA((2,))],
        ),
        out_shape=jax.ShapeDtypeStruct((n_rows, f), BF16),
        compiler_params=_params(2),
        name="moe_up",
    )(te, xt, rows, nxt, xs, w_gate, w_up)


def _moe_down_kernel(te_ref, xt_ref, rows_ref, nxt_ref, x_ref, w_ref, o_ref, stage_ref, wb_ref, sems, *, layer):
    del xt_ref
    j = pl.program_id(1)
    half = x_ref.shape[0] // 2
    _stream_expert_weights(te_ref, nxt_ref, (w_ref,), layer, stage_ref, wb_ref, sems, False)

    def half_tile(r0, used):
        @pl.when(used)
        def _compute():
            acc = jnp.dot(x_ref[r0:r0 + half, :], wb_ref[0], preferred_element_type=F32)
            o_ref[r0:r0 + half] = acc.astype(o_ref.dtype).reshape((half,) + o_ref.shape[1:])

        @pl.when(jnp.logical_not(used))
        def _unused():
            o_ref[r0:r0 + half] = jnp.zeros((half,) + o_ref.shape[1:], o_ref.dtype)

    half_tile(0, rows_ref[j] > 0)
    half_tile(half, rows_ref[j] > half)


def _moe_down(hid, w_down, layer, te, xt, rows, nxt, tm, bn):
    n_rows, f = hid.shape
    d = w_down.shape[3]
    n_tiles = n_rows // tm
    return pl.pallas_call(
        functools.partial(_moe_down_kernel, layer=layer),
        grid_spec=pltpu.PrefetchScalarGridSpec(
            num_scalar_prefetch=4,
            grid=(d // bn, n_tiles),
            in_specs=[
                pl.BlockSpec((tm, f), lambda jn, j, te, xt, rows, nxt: (xt[j], 0)),
                pl.BlockSpec(memory_space=pl.ANY),
            ],
            out_specs=pl.BlockSpec((tm, bn // LANES, LANES), lambda jn, j, te, xt, rows, nxt: (j, jn, 0)),
            scratch_shapes=[pltpu.VMEM((1, f, bn), F32), pltpu.VMEM((1, f, bn), BF16),
                            pltpu.SemaphoreType.DMA((1,))],
        ),
        out_shape=jax.ShapeDtypeStruct((n_rows, d // LANES, LANES), BF16),
        compiler_params=_params(2),
        name="moe_down",
    )(te, xt, rows, nxt, hid, w_down)


def _combine_ln_kernel(pos1_ref, pos2_ref, x_ref, rw_ref, mod_ref, lng_ref, lnb_ref, nmod_ref, ys_ref,
                       *rest, alpha, emit_h):
    if emit_h:
        xo_ref, h_ref, buf_ref, sem = rest
    else:
        xo_ref, buf_ref, sem = rest
    tm = x_ref.shape[0]
    i = pl.program_id(0)

    def gather(tile, slot):
        base = tile * tm

        def start(t, carry):
            _row_copy(ys_ref, pos1_ref[base + t], buf_ref.at[slot, 0], t, sem.at[slot]).start()
            _row_copy(ys_ref, pos2_ref[base + t], buf_ref.at[slot, 1], t, sem.at[slot]).start()
            return carry

        lax.fori_loop(0, tm, start, 0, unroll=8)

    @pl.when(i == 0)
    def _first_tile():
        gather(0, 0)

    @pl.when(i + 1 < pl.num_programs(0))
    def _prefetch():
        gather(i + 1, (i + 1) % 2)

    slot = i % 2
    for k in range(2):
        pltpu.make_async_copy(ys_ref.at[pl.ds(0, tm)], buf_ref.at[slot, k], sem.at[slot]).wait()
    rw = rw_ref[...]
    y1 = buf_ref[slot, 0].reshape(x_ref.shape).astype(F32)
    y2 = buf_ref[slot, 1].reshape(x_ref.shape).astype(F32)
    y = rw[:, 0:1] * y1 + rw[:, 1:2] * y2
    xn = _residual_ln(x_ref[...], y, mod_ref, lng_ref, lnb_ref, alpha, 5)
    xo_ref[...] = xn
    if emit_h:
        h_ref[...] = (xn * (1.0 + nmod_ref[1:2, :]) + nmod_ref[0:1, :]).astype(BF16)


def _combine_ln(x, ys, pos1, pos2, rw, mod, layer, next_layer, lng, lnb, seq, alpha, tm):
    t, d = x.shape
    tiles_per_seq = seq // tm
    emit_h = next_layer is not None
    tok = lambda i, p1, p2: (i, 0)
    const = lambda i, p1, p2: (0, 0)
    mod_spec = lambda l: pl.BlockSpec((None, None, N_MOD, d), lambda i, p1, p2: (l, i // tiles_per_seq, 0, 0))
    out_specs = [pl.BlockSpec((tm, d), tok)]
    out_shape = [jax.ShapeDtypeStruct((t, d), F32)]
    if emit_h:
        out_specs.append(pl.BlockSpec((tm, d), tok))
        out_shape.append(jax.ShapeDtypeStruct((t, d), BF16))
    outs = pl.pallas_call(
        functools.partial(_combine_ln_kernel, alpha=alpha, emit_h=emit_h),
        grid_spec=pltpu.PrefetchScalarGridSpec(
            num_scalar_prefetch=2,
            grid=(t // tm,),
            in_specs=[
                pl.BlockSpec((tm, d), tok),
                pl.BlockSpec((tm, LANES), tok),
                mod_spec(layer),
                pl.BlockSpec((1, d), const),
                pl.BlockSpec((1, d), const),
                mod_spec(next_layer if emit_h else layer),
                pl.BlockSpec(memory_space=pl.ANY),
            ],
            out_specs=out_specs,
            scratch_shapes=[pltpu.VMEM((2, 2, tm, d // LANES, LANES), BF16), pltpu.SemaphoreType.DMA((2,))],
        ),
        out_shape=out_shape,
        compiler_params=_params(1),
        name="moe_combine_ln",
    )(pos1, pos2, x, rw, mod, lng.reshape(1, d), lnb.reshape(1, d), mod, ys)
    return (outs[0], outs[1]) if emit_h else (outs[0], None)


def _tiles(t, seq, d, d_ff):
    return dict(
        bm=min(1024, seq),
        bn1=min(1024, d // 2),
        bn2=min(256, d // 2),
        bn_merge=min(512, d // 2),
        mod_bn=min(1024, d),
        conv_tm=min(256, seq),
        conv_b_tm=min(512, seq),
        ln_tm=min(256, seq),
        moe_tm=min(512, seq),
        moe_bf=min(512, d_ff),
        moe_bn=min(4096, d),
        dispatch_tm=min(512, seq),
    )


def _moe_schedule(counts, ranks, experts, tm, n_tiles):
    n_experts = counts.shape[0]
    tiles_per_expert = (counts + tm - 1) // tm
    tile_end = jnp.cumsum(tiles_per_expert)
    tile_start = tile_end - tiles_per_expert
    total = tile_end[-1]
    row_start = tile_start * tm
    pos = [row_start[experts[k]] + ranks[k] for k in range(2)]
    j = jnp.arange(n_tiles, dtype=I32)
    xt = jnp.minimum(j, total - 1)
    te = jnp.minimum(jnp.sum((xt[:, None] >= tile_end[None, :]).astype(I32), axis=1), n_experts - 1)
    rows = jnp.where(j < total, jnp.clip(counts[te] - (j - tile_start[te]) * tm, 0, tm), 0)
    eid = jnp.arange(n_experts, dtype=I32)
    later = jnp.where((eid[None, :] > eid[:, None]) & (tiles_per_expert[None, :] > 0), eid[None, :], n_experts)
    next_expert = jnp.min(later, axis=1)
    nxt = jnp.where(next_expert < n_experts, next_expert, -1)[te]
    return (pos[0].astype(I32), pos[1].astype(I32), te.astype(I32), xt.astype(I32), rows.astype(I32),
            nxt.astype(I32))


def kernel(x, c, w_ada, b_ada, w_in, conv_a_w, conv_a_b, ln_a_g, ln_a_b, conv_b_w, w_pa, w_pb, w_o,
           ln_mix_g, ln_mix_b, w_router, router_bias, w_gate, w_up, w_down, ln_ffn_g, ln_ffn_b):
    batch, seq, d = x.shape
    n_layers = w_in.shape[0]
    d_conf = conv_a_w.shape[2]
    d_short = conv_b_w.shape[2]
    n_experts, d_ff = w_gate.shape[1], w_gate.shape[3]
    t = batch * seq
    alpha = float((2 * n_layers) ** 0.25)
    ts = _tiles(t, seq, d, d_ff)
    bm, bn1, bn2 = ts["bm"], ts["bn1"], ts["bn2"]
    c_aval, c_agate = 0, d_conf
    c_sb, c_sc, c_sx = 2 * d_conf, 2 * d_conf + d_short, 2 * d_conf + 2 * d_short
    c_ga = 2 * d_conf + 3 * d_short
    n_moe_tiles = (2 * t) // ts["moe_tm"] + n_experts
    n_moe_rows = n_moe_tiles * ts["moe_tm"]

    mod = _modulation(c, w_ada, b_ada, ts["mod_bn"])
    xc = x.reshape(t, d)
    h = _modulate(xc, mod, 0, seq, ts["ln_tm"])
    xs = jnp.zeros((n_moe_rows, d // LANES, LANES), BF16)
    for l in range(n_layers):
        w_in_l = lambda col0: (w_in, l, col0, 0)
        ya_pre = _matmul([h], [w_in_l(c_aval), w_in_l(c_agate)], [], _ep_glu, d_conf, F32, bm, bn2, "inproj_glu")
        cx = _matmul([h], [w_in_l(c_sc), w_in_l(c_sx)], [], _ep_mul, d_short, F32, bm, bn2, "inproj_cx")
        sb = _matmul([h], [w_in_l(c_sb)], [], _ep_id, d_short, BF16, bm, bn1, "inproj_sb")
        gates = _matmul([h], [w_in_l(c_ga)], [], _ep_sigmoid, 2 * d, BF16, bm, bn1, "inproj_gates")
        ya = _causal_conv(ya_pre, conv_a_w[l], seq, ts["conv_tm"], "ln_silu",
                          (conv_a_b[l], ln_a_g[l], ln_a_b[l]), "conv_a")
        yb = _causal_conv(cx, conv_b_w[l], seq, ts["conv_b_tm"], "gate", (sb,), "conv_b")
        m = _matmul([ya, yb], [(w_pa, l, 0, 0), (w_pb, l, 0, 1)], [(gates, 0), (gates, d)],
                    _ep_merge, d, BF16, bm, ts["bn_merge"], "proj_merge")
        y = _matmul([m], [(w_o, l, 0, 0)], [], _ep_id, d, BF16, bm, bn1, "out_proj")
        xc, h2, ri, rw, cnt = _ln_router(xc, y, mod, l, ln_mix_g[l], ln_mix_b[l], w_router, router_bias,
                                         seq, alpha, ts["ln_tm"])
        pos1, pos2, te, xt, rows, nxt = _moe_schedule(cnt[:, 0], (ri[2], ri[3]), (ri[0], ri[1]),
                                                      ts["moe_tm"], n_moe_tiles)
        xs = _dispatch(h2, pos1, pos2, xs, ts["dispatch_tm"])
        hid = _moe_up(xs, w_gate, w_up, l, te, xt, rows, nxt, ts["moe_tm"], ts["moe_bf"])
        ys = _moe_down(hid, w_down, l, te, xt, rows, nxt, ts["moe_tm"], ts["moe_bn"])
        xc, h = _combine_ln(xc, ys, pos1, pos2, rw, mod, l, l + 1 if l + 1 < n_layers else None,
                            ln_ffn_g[l], ln_ffn_b[l], seq, alpha, ts["ln_tm"])
    return xc.reshape(batch, seq, d)
```
